```python
import jax, jax.numpy as jnp
from jax import lax
import numpy as np

D_MODEL = 4096
BATCH = 4
SEQ = 2048
DEPTH = 1
DEC_BATCH = 2
DEC_SEQ = 4096
PAST_LEN = 128

HEAD_DIM = 128
HEADS_PER_GROUP = 8
DILATED_GROUPS = ((128, 1), (512, 4), (2048, 16))
N_ATT_GROUPS = 3
ATT_QKV_WIDTH = N_ATT_GROUPS * HEADS_PER_GROUP * HEAD_DIM
ATT_OUT_WIDTH = HEADS_PER_GROUP * HEAD_DIM
ROPE_DIM = HEAD_DIM // 4
ROPE_THETA = 500000.0
NEG_INF = -1e30
SGU_WIDTH = D_MODEL
SGU_CHUNK = 128
SGU_GROUPS = 16
SGU_GROUP_DIM = SGU_WIDTH // SGU_GROUPS
D_FF = -(-8 * D_MODEL // (3 * 256)) * 256
IN_WIDTH = 3 * ATT_QKV_WIDTH + 2 * SGU_WIDTH + 2 * D_MODEL
NORM_EPS = 1e-6

kernel_name = "hybrid_dilated_attn_gmlp_encoder"


def rmsnorm(x, g):
    xf = x.astype(jnp.float32)
    y = xf * lax.rsqrt(jnp.mean(xf * xf, axis=-1, keepdims=True) + NORM_EPS)
    return (y * g.astype(jnp.float32)).astype(x.dtype)


def partial_rotary(x, pos):
    half = ROPE_DIM // 2
    inv_freq = ROPE_THETA ** (-jnp.arange(0, ROPE_DIM, 2, dtype=jnp.float32) / ROPE_DIM)
    ang = pos[:, None] * inv_freq[None, :]
    cos = jnp.cos(ang)[None, :, None, :]
    sin = jnp.sin(ang)[None, :, None, :]
    xf = x[..., :ROPE_DIM].astype(jnp.float32)
    x1, x2 = xf[..., :half], xf[..., half:]
    rot = jnp.concatenate([x1 * cos - x2 * sin, x2 * cos + x1 * sin], axis=-1)
    return jnp.concatenate([rot.astype(x.dtype), x[..., ROPE_DIM:]], axis=-1)


def dilated_window_attention(q, k, v, dil, radius):
    B, S, H, Dh = q.shape
    L = S // dil
    blk = radius
    nb = -(-L // blk)
    Lp = nb * blk
    def sub(t):
        return t.reshape(B, L, dil, H, Dh)
    qs = jnp.pad(sub(q), ((0, 0), (0, Lp - L), (0, 0), (0, 0), (0, 0))).reshape(B, nb, blk, dil, H, Dh)
    def windows(t):
        tp = jnp.pad(sub(t), ((0, 0), (blk, Lp - L + blk), (0, 0), (0, 0), (0, 0)))
        tb = tp.reshape(B, nb + 2, blk, dil, H, Dh)
        return jnp.concatenate([tb[:, :-2], tb[:, 1:-1], tb[:, 2:]], axis=2)
    kw = windows(k)
    vw = windows(v)
    scale = 1.0 / np.sqrt(Dh)
    s = jnp.einsum("bnqrhd,bnkrhd->bnrhqk", qs, kw, preferred_element_type=jnp.float32) * scale
    qpos = jnp.arange(nb)[:, None] * blk + jnp.arange(blk)[None, :]
    kpos = jnp.arange(nb)[:, None] * blk - blk + jnp.arange(3 * blk)[None, :]
    valid = ((jnp.abs(qpos[:, :, None] - kpos[:, None, :]) <= radius)
             & (kpos >= 0)[:, None, :] & (kpos < L)[:, None, :])
    s = jnp.where(valid[None, :, None, None], s, NEG_INF)
    m = jnp.max(s, axis=-1)
    p = jnp.exp(s - m[..., None])
    l = jnp.sum(p, axis=-1)
    o = jnp.einsum("bnrhqk,bnkrhd->bnqrhd", p, vw.astype(jnp.float32))
    o = o.reshape(B, Lp, dil, H, Dh)[:, :L].reshape(B, S, H, Dh)
    def back(t):
        return jnp.transpose(t, (0, 1, 4, 2, 3)).reshape(B, Lp, dil, H)[:, :L].reshape(B, S, H)
    return o, back(m), back(l)


def encoder_layer(x, attn_norm, w_in, sgu_norm, sgu_w, sgu_b, w_branch, w_out,
                  ffn_norm, w_gate, w_up, w_down):
    B, S, _ = x.shape
    h = rmsnorm(x, attn_norm)
    z = h @ w_in
    a = ATT_QKV_WIDTH
    splits = [a, 2 * a, 3 * a, 3 * a + SGU_WIDTH, 3 * a + 2 * SGU_WIDTH,
              3 * a + 2 * SGU_WIDTH + D_MODEL]
    q, k, v, u, vs, g_a, g_b = jnp.split(z, splits, axis=-1)
    q = q.reshape(B, S, N_ATT_GROUPS, HEADS_PER_GROUP, HEAD_DIM)
    k = k.reshape(B, S, N_ATT_GROUPS, HEADS_PER_GROUP, HEAD_DIM)
    v = v.reshape(B, S, N_ATT_GROUPS, HEADS_PER_GROUP, HEAD_DIM)
    pos = jnp.arange(S, dtype=jnp.float32)

    outs, maxes, dens = [], [], []
    for g, (window, dil) in enumerate(DILATED_GROUPS):
        radius = window // (2 * dil)
        o_g, m_g, l_g = dilated_window_attention(partial_rotary(q[:, :, g], pos),
                                                 partial_rotary(k[:, :, g], pos),
                                                 v[:, :, g], dil, radius)
        outs.append(o_g); maxes.append(m_g); dens.append(l_g)
    m_all = jnp.stack(maxes)
    wts = jnp.exp(m_all - jnp.max(m_all, axis=0, keepdims=True))
    num = jnp.sum(wts[..., None] * jnp.stack(outs), axis=0)
    den = jnp.sum(wts * jnp.stack(dens), axis=0)
    attn_out = (num / den[..., None]).reshape(B, S, ATT_OUT_WIDTH).astype(x.dtype)

    u = jax.nn.gelu(u, approximate=False)
    vs = rmsnorm(jax.nn.gelu(vs, approximate=False), sgu_norm)
    vc = vs.reshape(B, S // SGU_CHUNK, SGU_CHUNK, SGU_GROUPS, SGU_GROUP_DIM)
    sp = jnp.einsum("gts,bnsgc->bntgc", sgu_w, vc) + sgu_b.T[:, :, None]
    sgu_out = u * sp.reshape(B, S, SGU_WIDTH)

    branch_a = attn_out @ w_branch[:ATT_OUT_WIDTH]
    branch_b = sgu_out @ w_branch[ATT_OUT_WIDTH:]
    merged = jax.nn.sigmoid(g_a) * branch_a + jax.nn.sigmoid(g_b) * branch_b
    x = x + merged @ w_out

    h = rmsnorm(x, ffn_norm)
    x = x + (jax.nn.silu(h @ w_gate) * (h @ w_up)) @ w_down
    return x


def trunk(x, attn_norm, w_in, sgu_norm, sgu_w, sgu_b, w_branch, w_out,
          ffn_norm, w_gate, w_up, w_down, final_norm):
    for i in range(DEPTH):
        x = encoder_layer(x, attn_norm[i], w_in[i], sgu_norm[i], sgu_w[i], sgu_b[i],
                          w_branch[i], w_out[i], ffn_norm[i], w_gate[i], w_up[i], w_down[i])
    return rmsnorm(x, final_norm)


def setup_inputs(seed: int = 0) -> dict:
    key = jax.random.key(seed)
    ks = jax.random.split(key, 16)
    f32 = jnp.float32
    def nrm(k, shape, scale):
        return jax.random.normal(k, shape, f32) * scale
    w_branch = jnp.concatenate([
        nrm(ks[6], (DEPTH, ATT_OUT_WIDTH, D_MODEL), ATT_OUT_WIDTH ** -0.5),
        nrm(ks[7], (DEPTH, SGU_WIDTH, D_MODEL), SGU_WIDTH ** -0.5)], axis=1)
    return {
        "x_prompt": jax.random.normal(ks[0], (BATCH, SEQ, D_MODEL), f32),
        "x_sample": jax.random.normal(ks[1], (DEC_BATCH, DEC_SEQ, D_MODEL), f32),
        "attn_norm": 1.0 + nrm(ks[2], (DEPTH, D_MODEL), 0.02),
        "w_in": nrm(ks[3], (DEPTH, D_MODEL, IN_WIDTH), D_MODEL ** -0.5),
        "sgu_norm": 1.0 + nrm(ks[4], (DEPTH, SGU_WIDTH), 0.02),
        "sgu_w": nrm(ks[5], (DEPTH, SGU_GROUPS, SGU_CHUNK, SGU_CHUNK), SGU_CHUNK ** -0.5),
        "sgu_b": 1.0 + nrm(ks[8], (DEPTH, SGU_GROUPS, SGU_CHUNK), 0.02),
        "w_branch": w_branch,
        "w_out": nrm(ks[9], (DEPTH, D_MODEL, D_MODEL), D_MODEL ** -0.5),
        "ffn_norm": 1.0 + nrm(ks[10], (DEPTH, D_MODEL), 0.02),
        "w_gate": nrm(ks[11], (DEPTH, D_MODEL, D_FF), D_MODEL ** -0.5),
        "w_up": nrm(ks[12], (DEPTH, D_MODEL, D_FF), D_MODEL ** -0.5),
        "w_down": nrm(ks[13], (DEPTH, D_FF, D_MODEL), D_FF ** -0.5),
        "final_norm": 1.0 + nrm(ks[14], (D_MODEL,), 0.02),
    }


def reference(x_prompt, x_sample, attn_norm, w_in, sgu_norm, sgu_w, sgu_b, w_branch, w_out,
              ffn_norm, w_gate, w_up, w_down, final_norm):
    y_prompt = trunk(x_prompt, attn_norm, w_in, sgu_norm, sgu_w, sgu_b, w_branch, w_out,
                     ffn_norm, w_gate, w_up, w_down, final_norm)
    y_sample = trunk(x_sample, attn_norm, w_in, sgu_norm, sgu_w, sgu_b, w_branch, w_out,
                     ffn_norm, w_gate, w_up, w_down, final_norm)
    return (y_prompt, y_sample)
```

```python
import functools
import math

import jax
import jax.numpy as jnp
from jax import lax
from jax.experimental import pallas as pl
from jax.experimental.pallas import tpu as pltpu

F32 = jnp.float32
BF16 = jnp.bfloat16

NORM_EPS = 1e-6
HEAD_DIM = 128
HEADS_PER_GROUP = 8
GROUP_WIDTH = HEADS_PER_GROUP * HEAD_DIM
DILATED_GROUPS = ((128, 1), (512, 4), (2048, 16))
N_GROUPS = len(DILATED_GROUPS)
ROPE_DIM = HEAD_DIM // 4
ROPE_HALF = ROPE_DIM // 2
ROPE_THETA = 500000.0
NEG_INF = -1e30
SGU_CHUNK = 128
SGU_GROUPS = 16
LANES = 128
V7X_VMEM_BYTES = 64 * 1024 * 1024
VMEM_LIMIT_BYTES = V7X_VMEM_BYTES - 8 * 1024 * 1024


def _params():
    return pltpu.CompilerParams(vmem_limit_bytes=VMEM_LIMIT_BYTES)


def _rmsnorm_kernel(x_ref, g_ref, o_ref):
    x = x_ref[...]
    y = x * lax.rsqrt(jnp.mean(x * x, axis=-1, keepdims=True) + NORM_EPS)
    o_ref[...] = (y * g_ref[...]).astype(o_ref.dtype)


def _rmsnorm(x, g, out_dtype, name):
    t, d = x.shape
    tr = 256
    return pl.pallas_call(
        _rmsnorm_kernel,
        grid=(t // tr,),
        in_specs=[pl.BlockSpec((tr, d), lambda i: (i, 0)),
                  pl.BlockSpec((1, d), lambda i: (0, 0))],
        out_specs=pl.BlockSpec((tr, d), lambda i: (i, 0)),
        out_shape=jax.ShapeDtypeStruct((t, d), out_dtype),
        compiler_params=_params(),
        name=name,
    )(x, g.reshape(1, d))


def _rope_table_kernel(f_ref, c_ref, sa_ref, sb_ref):
    rows = c_ref.shape[0]
    pos = (pl.program_id(0) * rows + lax.broadcasted_iota(jnp.int32, (rows, LANES), 0)).astype(F32)
    lane = lax.broadcasted_iota(jnp.int32, (rows, LANES), 1)
    ang = pos * f_ref[...]
    sin = jnp.sin(ang)
    c_ref[...] = jnp.cos(ang)
    sa_ref[...] = jnp.where(lane < ROPE_HALF, -sin, 0.0)
    sb_ref[...] = jnp.where((lane >= ROPE_HALF) & (lane < ROPE_DIM), sin, 0.0)


def _rope_tables(seq):
    inv_freq = ROPE_THETA ** (-jnp.arange(0, ROPE_DIM, 2, dtype=F32) / ROPE_DIM)
    lane_freq = jnp.concatenate([inv_freq, inv_freq, jnp.zeros((LANES - ROPE_DIM,), F32)]).reshape(1, LANES)
    rows = 256
    spec = pl.BlockSpec((rows, LANES), lambda i: (i, 0))
    shape = jax.ShapeDtypeStruct((seq, LANES), F32)
    return pl.pallas_call(
        _rope_table_kernel,
        grid=(seq // rows,),
        in_specs=[pl.BlockSpec((1, LANES), lambda i: (0, 0))],
        out_specs=[spec, spec, spec],
        out_shape=[shape, shape, shape],
        name="rope_tables",
    )(lane_freq)


def _gelu(x):
    return 0.5 * x * (1.0 + lax.erf(x * math.sqrt(0.5)))


def _proj_kernel(a_ref, w_ref, *rest, kind):
    o_ref = rest[-1]
    acc = jnp.dot(a_ref[...], w_ref[...], preferred_element_type=F32)
    if kind == "rope":
        c_ref, sa_ref, sb_ref = rest[:3]
        c, sa, sb = c_ref[...], sa_ref[...], sb_ref[...]
        for h in range(acc.shape[1] // HEAD_DIM):
            sl = slice(h * HEAD_DIM, (h + 1) * HEAD_DIM)
            z = acc[:, sl]
            rot = z * c + pltpu.roll(z, HEAD_DIM - ROPE_HALF, 1) * sa + pltpu.roll(z, ROPE_HALF, 1) * sb
            o_ref[:, sl] = rot.astype(o_ref.dtype)
    elif kind == "gelu":
        o_ref[...] = _gelu(acc).astype(o_ref.dtype)
    elif kind == "sigmoid":
        o_ref[...] = jax.nn.sigmoid(acc).astype(o_ref.dtype)
    else:
        o_ref[...] = acc.astype(o_ref.dtype)


def _proj(a, w, col0, ncols, kind, name, tables=None, seq=None):
    m, k = a.shape
    tm, tn = 1024, 1024
    in_specs = [pl.BlockSpec((tm, k), lambda i, j: (i, 0)),
                pl.BlockSpec((k, tn), lambda i, j: (0, col0 // tn + j))]
    args = [a, w]
    if kind == "rope":
        blocks_per_seq = seq // tm
        tspec = pl.BlockSpec((tm, LANES), lambda i, j: (i % blocks_per_seq, 0))
        in_specs += [tspec, tspec, tspec]
        args += list(tables)
    return pl.pallas_call(
        functools.partial(_proj_kernel, kind=kind),
        grid=(m // tm, ncols // tn),
        in_specs=in_specs,
        out_specs=pl.BlockSpec((tm, tn), lambda i, j: (i, j)),
        out_shape=jax.ShapeDtypeStruct((m, ncols), BF16),
        compiler_params=_params(),
        name=name,
    )(*args)


def _attn_kernel(q_ref, k_ref, v_ref, o_ref, m_ref, l_ref, *, tq, win, sub_len, radius):
    l0 = pl.program_id(2) * tq
    start = pl.multiple_of(jnp.clip(l0 - radius, 0, sub_len - win), radius)
    qpos = l0 + lax.broadcasted_iota(jnp.int32, (tq, win), 0)
    kpos = start + lax.broadcasted_iota(jnp.int32, (tq, win), 1)
    valid = jnp.abs(qpos - kpos) <= radius
    lane = lax.broadcasted_iota(jnp.int32, (tq, LANES), 1)
    scale = 1.0 / math.sqrt(HEAD_DIM)
    m_all = jnp.zeros((tq, LANES), F32)
    l_all = jnp.zeros((tq, LANES), F32)
    for h in range(HEADS_PER_GROUP):
        sl = slice(h * HEAD_DIM, (h + 1) * HEAD_DIM)
        q = q_ref[:, sl]
        k = k_ref[pl.ds(start, win), sl]
        v = v_ref[pl.ds(start, win), sl]
        s = lax.dot_general(q, k, (((1,), (1,)), ((), ())), preferred_element_type=F32) * scale
        s = jnp.where(valid, s, NEG_INF)
        m = jnp.max(s, axis=-1, keepdims=True)
        p = jnp.exp(s - m)
        l = jnp.sum(p, axis=-1, keepdims=True)
        o = jnp.dot(p.astype(BF16), v, preferred_element_type=F32)
        o_ref[:, sl] = o.astype(o_ref.dtype)
        m_all = jnp.where(lane == h, m, m_all)
        l_all = jnp.where(lane == h, l, l_all)
    m_ref[...] = m_all
    l_ref[...] = l_all


def _attn_group(qk, v, batch, seq, g):
    window, dil = DILATED_GROUPS[g]
    radius = window // (2 * dil)
    sub_len = seq // dil
    tq = min(256, sub_len)
    win = min(tq + 2 * radius, sub_len)
    qk_cols = 2 * N_GROUPS
    qk3 = qk.reshape(batch, sub_len, dil * qk_cols * GROUP_WIDTH)
    v3 = v.reshape(batch, sub_len, dil * N_GROUPS * GROUP_WIDTH)
    o, m, l = pl.pallas_call(
        functools.partial(_attn_kernel, tq=tq, win=win, sub_len=sub_len, radius=radius),
        grid=(batch, dil, sub_len // tq),
        in_specs=[
            pl.BlockSpec((None, tq, GROUP_WIDTH), lambda b, r, j: (b, j, r * qk_cols + g)),
            pl.BlockSpec((None, sub_len, GROUP_WIDTH), lambda b, r, j: (b, 0, r * qk_cols + N_GROUPS + g)),
            pl.BlockSpec((None, sub_len, GROUP_WIDTH), lambda b, r, j: (b, 0, r * N_GROUPS + g)),
        ],
        out_specs=[
            pl.BlockSpec((None, tq, GROUP_WIDTH), lambda b, r, j: (b, j, r)),
            pl.BlockSpec((None, tq, LANES), lambda b, r, j: (b, j, r)),
            pl.BlockSpec((None, tq, LANES), lambda b, r, j: (b, j, r)),
        ],
        out_shape=[
            jax.ShapeDtypeStruct((batch, sub_len, dil * GROUP_WIDTH), BF16),
            jax.ShapeDtypeStruct((batch, sub_len, dil * LANES), F32),
            jax.ShapeDtypeStruct((batch, sub_len, dil * LANES), F32),
        ],
        compiler_params=_params(),
        name=f"attn_g{g}",
    )(qk3, qk3, v3)
    t = batch * seq
    return o.reshape(t, GROUP_WIDTH), m.reshape(t, LANES), l.reshape(t, LANES)


def _combine_kernel(*refs):
    o_refs, m_refs, l_refs, out_ref = refs[0:3], refs[3:6], refs[6:9], refs[9]
    ms = [r[...] for r in m_refs]
    m_max = jnp.maximum(jnp.maximum(ms[0], ms[1]), ms[2])
    ws = [jnp.exp(m - m_max) for m in ms]
    den = ws[0] * l_refs[0][...] + ws[1] * l_refs[1][...] + ws[2] * l_refs[2][...]
    for h in range(HEADS_PER_GROUP):
        sl = slice(h * HEAD_DIM, (h + 1) * HEAD_DIM)
        num = (ws[0][:, h:h + 1] * o_refs[0][:, sl].astype(F32)
               + ws[1][:, h:h + 1] * o_refs[1][:, sl].astype(F32)
               + ws[2][:, h:h + 1] * o_refs[2][:, sl].astype(F32))
        out_ref[:, sl] = (num / den[:, h:h + 1]).astype(out_ref.dtype)


def _combine(os_, ms, ls):
    t = os_[0].shape[0]
    tr = 512
    ospec = pl.BlockSpec((tr, GROUP_WIDTH), lambda i: (i, 0))
    sspec = pl.BlockSpec((tr, LANES), lambda i: (i, 0))
    return pl.pallas_call(
        _combine_kernel,
        grid=(t // tr,),
        in_specs=[ospec] * 3 + [sspec] * 6,
        out_specs=ospec,
        out_shape=jax.ShapeDtypeStruct((t, GROUP_WIDTH), BF16),
        compiler_params=_params(),
        name="attn_combine",
    )(*os_, *ms, *ls)


def _sgu_kernel(u_ref, vs_ref, g_ref, w_ref, bt_ref, o_ref):
    rows, width = vs_ref.shape
    gdim = width // SGU_GROUPS
    gv = vs_ref[...].astype(F32)
    vn = gv * lax.rsqrt(jnp.mean(gv * gv, axis=-1, keepdims=True) + NORM_EPS)
    vn = (vn * g_ref[...]).astype(BF16)
    for g in range(SGU_GROUPS):
        w = w_ref[g].astype(BF16)
        bias = bt_ref[:, g:g + 1]
        cs = slice(g * gdim, (g + 1) * gdim)
        for c in range(rows // SGU_CHUNK):
            rs = slice(c * SGU_CHUNK, (c + 1) * SGU_CHUNK)
            sp = jnp.dot(w, vn[rs, cs], preferred_element_type=F32) + bias
            o_ref[rs, cs] = (u_ref[rs, cs].astype(F32) * sp).astype(o_ref.dtype)


def _sgu(uv, sgu_norm, sgu_w, sgu_b):
    t = uv.shape[0]
    width = sgu_norm.shape[0]
    tr = 2 * SGU_CHUNK
    return pl.pallas_call(
        _sgu_kernel,
        grid=(t // tr,),
        in_specs=[
            pl.BlockSpec((tr, width), lambda i: (i, 0)),
            pl.BlockSpec((tr, width), lambda i: (i, 1)),
            pl.BlockSpec((1, width), lambda i: (0, 0)),
            pl.BlockSpec((SGU_GROUPS, SGU_CHUNK, SGU_CHUNK), lambda i: (0, 0, 0)),
            pl.BlockSpec((SGU_CHUNK, SGU_GROUPS), lambda i: (0, 0)),
        ],
        out_specs=pl.BlockSpec((tr, width), lambda i: (i, 0)),
        out_shape=jax.ShapeDtypeStruct((t, width), BF16),
        compiler_params=_params(),
        name="sgu",
    )(uv, uv, sgu_norm.reshape(1, width), sgu_w, sgu_b.T)


def _merge_kernel(at_ref, sg_ref, wa_ref, wb_ref, ga_ref, gb_ref, o_ref):
    a = jnp.dot(at_ref[...], wa_ref[...], preferred_element_type=F32)
    b = jnp.dot(sg_ref[...], wb_ref[...], preferred_element_type=F32)
    o_ref[...] = (ga_ref[...].astype(F32) * a + gb_ref[...].astype(F32) * b).astype(o_ref.dtype)


def _merge(attn, sgu, wa, wb, gates):
    t, d = attn.shape[0], wa.shape[1]
    tm, tn = 1024, 512
    return pl.pallas_call(
        _merge_kernel,
        grid=(t // tm, d // tn),
        in_specs=[
            pl.BlockSpec((tm, attn.shape[1]), lambda i, j: (i, 0)),
            pl.BlockSpec((tm, sgu.shape[1]), lambda i, j: (i, 0)),
            pl.BlockSpec((wa.shape[0], tn), lambda i, j: (0, j)),
            pl.BlockSpec((wb.shape[0], tn), lambda i, j: (0, j)),
            pl.BlockSpec((tm, tn), lambda i, j: (i, j)),
            pl.BlockSpec((tm, tn), lambda i, j: (i, d // tn + j)),
        ],
        out_specs=pl.BlockSpec((tm, tn), lambda i, j: (i, j)),
        out_shape=jax.ShapeDtypeStruct((t, d), BF16),
        compiler_params=_params(),
        name="branch_merge",
    )(attn, sgu, wa, wb, gates, gates)


def _residual_mm_kernel(a_ref, w_ref, x_ref, o_ref):
    o_ref[...] = x_ref[...] + jnp.dot(a_ref[...], w_ref[...], preferred_element_type=F32)


def _residual_mm(a, w, x, tm, tn, name):
    t, k = a.shape
    d = w.shape[1]
    return pl.pallas_call(
        _residual_mm_kernel,
        grid=(t // tm, d // tn),
        in_specs=[
            pl.BlockSpec((tm, k), lambda i, j: (i, 0)),
            pl.BlockSpec((k, tn), lambda i, j: (0, j)),
            pl.BlockSpec((tm, tn), lambda i, j: (i, j)),
        ],
        out_specs=pl.BlockSpec((tm, tn), lambda i, j: (i, j)),
        out_shape=jax.ShapeDtypeStruct((t, d), F32),
        compiler_params=_params(),
        name=name,
    )(a, w, x)


def _gate_up_kernel(h_ref, wg_ref, wu_ref, o_ref):
    h = h_ref[...]
    g = jnp.dot(h, wg_ref[...], preferred_element_type=F32)
    u = jnp.dot(h, wu_ref[...], preferred_element_type=F32)
    o_ref[...] = (jax.nn.silu(g) * u).astype(o_ref.dtype)


def _gate_up(h, wg, wu):
    t, k = h.shape
    f = wg.shape[1]
    tm, tn = 1024, 256
    return pl.pallas_call(
        _gate_up_kernel,
        grid=(t // tm, f // tn),
        in_specs=[
            pl.BlockSpec((tm, k), lambda i, j: (i, 0)),
            pl.BlockSpec((k, tn), lambda i, j: (0, j)),
            pl.BlockSpec((k, tn), lambda i, j: (0, j)),
        ],
        out_specs=pl.BlockSpec((tm, tn), lambda i, j: (i, j)),
        out_shape=jax.ShapeDtypeStruct((t, f), BF16),
        compiler_params=_params(),
        name="ffn_gate_up",
    )(h, wg, wu)


def _layer(x2, w, tables, batch, seq):
    d = x2.shape[1]
    qk_cols = 2 * N_GROUPS * GROUP_WIDTH
    v_cols = N_GROUPS * GROUP_WIDTH
    sgu_width = w["sgu_norm"].shape[0]

    h = _rmsnorm(x2, w["attn_norm"], BF16, "attn_rmsnorm")
    qk = _proj(h, w["w_in"], 0, qk_cols, "rope", "proj_qk", tables=tables, seq=seq)
    v = _proj(h, w["w_in"], qk_cols, v_cols, "none", "proj_v")
    uv = _proj(h, w["w_in"], qk_cols + v_cols, 2 * sgu_width, "gelu", "proj_uv")
    gates = _proj(h, w["w_in"], qk_cols + v_cols + 2 * sgu_width, 2 * d, "sigmoid", "proj_gates")

    parts = [_attn_group(qk, v, batch, seq, g) for g in range(N_GROUPS)]
    attn = _combine([p[0] for p in parts], [p[1] for p in parts], [p[2] for p in parts])
    sgu = _sgu(uv, w["sgu_norm"], w["sgu_w"], w["sgu_b"])
    merged = _merge(attn, sgu, w["wb_a"], w["wb_b"], gates)
    x1 = _residual_mm(merged, w["w_out"], x2, 1024, 512, "out_proj")

    h2 = _rmsnorm(x1, w["ffn_norm"], BF16, "ffn_rmsnorm")
    gu = _gate_up(h2, w["w_gate"], w["w_up"])
    return _residual_mm(gu, w["w_down"], x1, 512, 512, "ffn_down")


def kernel(x_prompt, x_sample, attn_norm, w_in, sgu_norm, sgu_w, sgu_b, w_branch, w_out,
           ffn_norm, w_gate, w_up, w_down, final_norm):
    depth = w_in.shape[0]
    layers = [
        {
            "attn_norm": attn_norm[i], "w_in": w_in[i].astype(BF16),
            "sgu_norm": sgu_norm[i], "sgu_w": sgu_w[i], "sgu_b": sgu_b[i],
            "wb_a": w_branch[i, :GROUP_WIDTH].astype(BF16), "wb_b": w_branch[i, GROUP_WIDTH:].astype(BF16),
            "w_out": w_out[i].astype(BF16), "ffn_norm": ffn_norm[i],
            "w_gate": w_gate[i].astype(BF16), "w_up": w_up[i].astype(BF16),
            "w_down": w_down[i].astype(BF16),
        }
        for i in range(depth)
    ]
    tables = _rope_tables(max(x_prompt.shape[1], x_sample.shape[1]))
    outs = []
    for x in (x_prompt, x_sample):
        batch, seq, d = x.shape
        x2 = x.reshape(batch * seq, d)
        for w in layers:
            x2 = _layer(x2, w, tables, batch, seq)
        outs.append(_rmsnorm(x2, final_norm, F32, "final_rmsnorm").reshape(batch, seq, d))
    return tuple(outs)
```

```python
import functools
import math

import jax
import jax.numpy as jnp
from jax import lax
from jax.experimental import pallas as pl
from jax.experimental.pallas import tpu as pltpu

F32 = jnp.float32
BF16 = jnp.bfloat16

NORM_EPS = 1e-6
HEAD_DIM = 128
HEADS_PER_GROUP = 8
GROUP_WIDTH = HEADS_PER_GROUP * HEAD_DIM
DILATED_GROUPS = ((128, 1), (512, 4), (2048, 16))
N_GROUPS = len(DILATED_GROUPS)
DILATIONS = tuple(sorted({d for _, d in DILATED_GROUPS}))
ROPE_DIM = HEAD_DIM // 4
ROPE_HALF = ROPE_DIM // 2
ROPE_THETA = 500000.0
NEG_INF = -1e30
SGU_CHUNK = 128
SGU_GROUPS = 16
LANES = 128
BF16_SUBLANES = 16
V7X_VMEM_BYTES = 64 * 1024 * 1024
VMEM_LIMIT_BYTES = V7X_VMEM_BYTES - 8 * 1024 * 1024


def _params():
    return pltpu.CompilerParams(vmem_limit_bytes=VMEM_LIMIT_BYTES)


def _to_residue_major(rows, dil):
    n = rows // dil
    i = lax.broadcasted_iota(jnp.int32, (rows, rows), 0)
    j = lax.broadcasted_iota(jnp.int32, (rows, rows), 1)
    src = (i % n) * dil + i // n
    return jnp.where(j == src, 1.0, 0.0).astype(BF16)


def _to_natural(rows, dil):
    n = rows // dil
    i = lax.broadcasted_iota(jnp.int32, (rows, rows), 0)
    j = lax.broadcasted_iota(jnp.int32, (rows, rows), 1)
    src = (i % dil) * n + i // dil
    return jnp.where(j == src, 1.0, 0.0).astype(BF16)


def _rmsnorm_kernel(x_ref, g_ref, o_ref):
    x = x_ref[...]
    y = x * lax.rsqrt(jnp.mean(x * x, axis=-1, keepdims=True) + NORM_EPS)
    o_ref[...] = (y * g_ref[...]).astype(o_ref.dtype)


def _rmsnorm(x, g, out_dtype, name):
    t, d = x.shape
    tr = 256
    return pl.pallas_call(
        _rmsnorm_kernel,
        grid=(t // tr,),
        in_specs=[pl.BlockSpec((tr, d), lambda i: (i, 0)),
                  pl.BlockSpec((1, d), lambda i: (0, 0))],
        out_specs=pl.BlockSpec((tr, d), lambda i: (i, 0)),
        out_shape=jax.ShapeDtypeStruct((t, d), out_dtype),
        compiler_params=_params(),
        name=name,
    )(x, g.reshape(1, d))


def _rmsnorm_orders_kernel(x_ref, g_ref, *o_refs):
    rows = x_ref.shape[0]
    x = x_ref[...]
    y = x * lax.rsqrt(jnp.mean(x * x, axis=-1, keepdims=True) + NORM_EPS)
    y = (y * g_ref[...]).astype(BF16)
    for dil, o_ref in zip(DILATIONS, o_refs):
        if dil == 1:
            o_ref[...] = y
        else:
            n = rows // dil
            yp = jnp.dot(_to_residue_major(rows, dil), y, preferred_element_type=F32).astype(BF16)
            for r in range(dil):
                o_ref[r] = yp[r * n:(r + 1) * n, :]


def _rmsnorm_orders(x2, g, batch, seq):
    t, d = x2.shape
    tr = BF16_SUBLANES * max(DILATIONS)
    per_seq = seq // tr
    out_specs, out_shape = [], []
    for dil in DILATIONS:
        if dil == 1:
            out_specs.append(pl.BlockSpec((tr, d), lambda i: (i, 0)))
            out_shape.append(jax.ShapeDtypeStruct((t, d), BF16))
        else:
            out_specs.append(pl.BlockSpec((None, dil, tr // dil, d),
                                          lambda i: (i // per_seq, 0, i % per_seq, 0)))
            out_shape.append(jax.ShapeDtypeStruct((batch, dil, seq // dil, d), BF16))
    outs = pl.pallas_call(
        _rmsnorm_orders_kernel,
        grid=(t // tr,),
        in_specs=[pl.BlockSpec((tr, d), lambda i: (i, 0)),
                  pl.BlockSpec((1, d), lambda i: (0, 0))],
        out_specs=out_specs,
        out_shape=out_shape,
        compiler_params=_params(),
        name="attn_rmsnorm",
    )(x2, g.reshape(1, d))
    return {dil: o.reshape(t, d) for dil, o in zip(DILATIONS, outs)}


def _rope_table_kernel(f_ref, c_ref, sa_ref, sb_ref, *, dil):
    rows = c_ref.shape[0]
    sub = pl.program_id(1) * rows + lax.broadcasted_iota(jnp.int32, (rows, LANES), 0)
    pos = (sub * dil + pl.program_id(0)).astype(F32)
    lane = lax.broadcasted_iota(jnp.int32, (rows, LANES), 1)
    ang = pos * f_ref[...]
    sin = jnp.sin(ang)
    c_ref[...] = jnp.cos(ang)
    sa_ref[...] = jnp.where(lane < ROPE_HALF, -sin, 0.0)
    sb_ref[...] = jnp.where((lane >= ROPE_HALF) & (lane < ROPE_DIM), sin, 0.0)


def _rope_tables(seq, dil):
    inv_freq = ROPE_THETA ** (-jnp.arange(0, ROPE_DIM, 2, dtype=F32) / ROPE_DIM)
    lane_freq = jnp.concatenate([inv_freq, inv_freq, jnp.zeros((LANES - ROPE_DIM,), F32)]).reshape(1, LANES)
    rows = 128
    blocks = seq // dil // rows
    spec = pl.BlockSpec((rows, LANES), lambda r, i: (r * blocks + i, 0))
    shape = jax.ShapeDtypeStruct((seq, LANES), F32)
    return pl.pallas_call(
        functools.partial(_rope_table_kernel, dil=dil),
        grid=(dil, blocks),
        in_specs=[pl.BlockSpec((1, LANES), lambda r, i: (0, 0))],
        out_specs=[spec, spec, spec],
        out_shape=[shape, shape, shape],
        name=f"rope_tables_d{dil}",
    )(lane_freq)


def _gelu(x):
    return 0.5 * x * (1.0 + lax.erf(x * math.sqrt(0.5)))


def _store_rope(acc, c_ref, sa_ref, sb_ref, o_ref):
    c, sa, sb = c_ref[...], sa_ref[...], sb_ref[...]
    for h in range(acc.shape[1] // HEAD_DIM):
        sl = slice(h * HEAD_DIM, (h + 1) * HEAD_DIM)
        z = acc[:, sl]
        rot = z * c + pltpu.roll(z, HEAD_DIM - ROPE_HALF, 1) * sa + pltpu.roll(z, ROPE_HALF, 1) * sb
        o_ref[:, sl] = rot.astype(o_ref.dtype)


def _proj_kernel(a_ref, w_ref, *rest, kind):
    o_ref = rest[-1]
    acc = jnp.dot(a_ref[...], w_ref[...], preferred_element_type=F32)
    if kind == "qkv":
        j = pl.program_id(1)

        @pl.when(j < 2)
        def _():
            _store_rope(acc, *rest[:3], o_ref)

        @pl.when(j == 2)
        def _():
            o_ref[...] = acc.astype(o_ref.dtype)
    elif kind == "gelu":
        o_ref[...] = _gelu(acc).astype(o_ref.dtype)
    else:
        o_ref[...] = jax.nn.sigmoid(acc).astype(o_ref.dtype)


def _proj(a, w, first, step, count, kind, name, tables=None, seq=None):
    m, k = a.shape
    tm, tn = 1024, GROUP_WIDTH
    in_specs = [pl.BlockSpec((tm, k), lambda i, j: (i, 0)),
                pl.BlockSpec((k, tn), lambda i, j: (0, first + step * j))]
    args = [a, w]
    if kind == "qkv":
        per_seq = seq // tm
        tspec = pl.BlockSpec((tm, LANES), lambda i, j: (i % per_seq, 0))
        in_specs += [tspec, tspec, tspec]
        args += list(tables)
    return pl.pallas_call(
        functools.partial(_proj_kernel, kind=kind),
        grid=(m // tm, count),
        in_specs=in_specs,
        out_specs=pl.BlockSpec((tm, tn), lambda i, j: (i, j)),
        out_shape=jax.ShapeDtypeStruct((m, count * tn), BF16),
        compiler_params=_params(),
        name=name,
    )(*args)


def _attn_kernel(q_ref, k_ref, v_ref, o_ref, m_ref, l_ref, *, qs, win, sub_len, radius):
    tq = q_ref.shape[0]
    kv_rows = k_ref.shape[0]
    rel = (lax.broadcasted_iota(jnp.int32, (qs, win), 0) - lax.broadcasted_iota(jnp.int32, (qs, win), 1))
    lane = lax.broadcasted_iota(jnp.int32, (qs, LANES), 1)
    scale = 1.0 / math.sqrt(HEAD_DIM)
    for qb in range(tq // qs):
        rows = slice(qb * qs, (qb + 1) * qs)
        if kv_rows == sub_len:
            l0 = pl.program_id(1) * tq + qb * qs
            start = pl.multiple_of(jnp.clip(l0 - radius, 0, sub_len - win), radius)
            kv = pl.ds(start, win)
        else:
            l0 = (qb * qs) % sub_len
            start = min(max(l0 - radius, 0), sub_len - win)
            kv = pl.ds((qb * qs) // sub_len * sub_len + start, win)
        valid = jnp.abs(rel + (l0 - start)) <= radius
        m_all = jnp.zeros((qs, LANES), F32)
        l_all = jnp.zeros((qs, LANES), F32)
        for h in range(HEADS_PER_GROUP):
            sl = slice(h * HEAD_DIM, (h + 1) * HEAD_DIM)
            s = lax.dot_general(q_ref[rows, sl], k_ref[kv, sl], (((1,), (1,)), ((), ())),
                                preferred_element_type=F32) * scale
            s = jnp.where(valid, s, NEG_INF)
            m = jnp.max(s, axis=-1, keepdims=True)
            p = jnp.exp(s - m)
            l = jnp.sum(p, axis=-1, keepdims=True)
            o = jnp.dot(p.astype(BF16), v_ref[kv, sl], preferred_element_type=F32)
            o_ref[rows, sl] = o.astype(o_ref.dtype)
            m_all = jnp.where(lane == h, m, m_all)
            l_all = jnp.where(lane == h, l, l_all)
        m_ref[rows, :] = m_all
        l_ref[rows, :] = l_all


def _attn_group(qkv, seq, g):
    t = qkv.shape[0]
    window, dil = DILATED_GROUPS[g]
    radius = window // (2 * dil)
    sub_len = seq // dil
    tq = 256
    qs = min(128, sub_len)
    win = min(qs + 2 * radius, sub_len)
    if sub_len >= tq:
        per_sub = sub_len // tq
        grid = (t // sub_len, per_sub)
        kv_rows = sub_len
        q_map = lambda s, j: (s * per_sub + j, 0)
        k_map = lambda s, j: (s, 1)
        v_map = lambda s, j: (s, 2)
    else:
        grid = (t // tq, 1)
        kv_rows = tq
        q_map = lambda s, j: (s, 0)
        k_map = lambda s, j: (s, 1)
        v_map = lambda s, j: (s, 2)
    return pl.pallas_call(
        functools.partial(_attn_kernel, qs=qs, win=win, sub_len=sub_len, radius=radius),
        grid=grid,
        in_specs=[
            pl.BlockSpec((tq, GROUP_WIDTH), q_map),
            pl.BlockSpec((kv_rows, GROUP_WIDTH), k_map),
            pl.BlockSpec((kv_rows, GROUP_WIDTH), v_map),
        ],
        out_specs=[
            pl.BlockSpec((tq, GROUP_WIDTH), q_map),
            pl.BlockSpec((tq, LANES), q_map),
            pl.BlockSpec((tq, LANES), q_map),
        ],
        out_shape=[
            jax.ShapeDtypeStruct((t, GROUP_WIDTH), BF16),
            jax.ShapeDtypeStruct((t, LANES), F32),
            jax.ShapeDtypeStruct((t, LANES), F32),
        ],
        compiler_params=_params(),
        name=f"attn_g{g}",
    )(qkv, qkv, qkv)


def _combine_kernel(*refs):
    o_refs, m_refs, l_refs = refs[0:N_GROUPS], refs[N_GROUPS:2 * N_GROUPS], refs[2 * N_GROUPS:3 * N_GROUPS]
    out_ref, m_scr, l_scr = refs[3 * N_GROUPS:]
    rows = out_ref.shape[0]
    os_, ms, ls = [], [], []
    for g, (_, dil) in enumerate(DILATED_GROUPS):
        if dil == 1:
            os_.append(o_refs[g][...].astype(F32))
            ms.append(m_refs[g][...])
            ls.append(l_refs[g][...])
        else:
            n = rows // dil
            for r in range(dil):
                m_scr[g, pl.ds(r, n, stride=dil), :] = m_refs[g][r]
                l_scr[g, pl.ds(r, n, stride=dil), :] = l_refs[g][r]
            ms.append(m_scr[g])
            ls.append(l_scr[g])
            og = o_refs[g][...].reshape(rows, GROUP_WIDTH)
            os_.append(jnp.dot(_to_natural(rows, dil), og, preferred_element_type=F32))
    m_max = functools.reduce(jnp.maximum, ms)
    ws = [jnp.exp(m - m_max) for m in ms]
    den = sum(w * l for w, l in zip(ws, ls))
    for h in range(HEADS_PER_GROUP):
        sl = slice(h * HEAD_DIM, (h + 1) * HEAD_DIM)
        num = sum(w[:, h:h + 1] * o[:, sl] for w, o in zip(ws, os_))
        out_ref[:, sl] = (num / den[:, h:h + 1]).astype(out_ref.dtype)


def _combine(parts, batch, seq):
    t = batch * seq
    tr = BF16_SUBLANES * 2 * max(DILATIONS)
    per_seq = seq // tr
    o_specs, s_specs, o_args, m_args, l_args = [], [], [], [], []
    for (o, m, l), (_, dil) in zip(parts, DILATED_GROUPS):
        if dil == 1:
            o_specs.append(pl.BlockSpec((tr, GROUP_WIDTH), lambda i: (i, 0)))
            s_specs.append(pl.BlockSpec((tr, LANES), lambda i: (i, 0)))
            o_args.append(o), m_args.append(m), l_args.append(l)
        else:
            imap = lambda i: (i // per_seq, 0, i % per_seq, 0)
            o_specs.append(pl.BlockSpec((None, dil, tr // dil, GROUP_WIDTH), imap))
            s_specs.append(pl.BlockSpec((None, dil, tr // dil, LANES), imap))
            o_args.append(o.reshape(batch, dil, seq // dil, GROUP_WIDTH))
            m_args.append(m.reshape(batch, dil, seq // dil, LANES))
            l_args.append(l.reshape(batch, dil, seq // dil, LANES))
    return pl.pallas_call(
        _combine_kernel,
        grid=(t // tr,),
        in_specs=o_specs + s_specs + s_specs,
        out_specs=pl.BlockSpec((tr, GROUP_WIDTH), lambda i: (i, 0)),
        out_shape=jax.ShapeDtypeStruct((t, GROUP_WIDTH), BF16),
        scratch_shapes=[pltpu.VMEM((N_GROUPS, tr, LANES), F32), pltpu.VMEM((N_GROUPS, tr, LANES), F32)],
        compiler_params=_params(),
        name="attn_combine",
    )(*o_args, *m_args, *l_args)


def _sgu_kernel(u_ref, vs_ref, g_ref, w_ref, bt_ref, o_ref):
    rows, width = vs_ref.shape
    gdim = width // SGU_GROUPS
    gv = vs_ref[...].astype(F32)
    vn = gv * lax.rsqrt(jnp.mean(gv * gv, axis=-1, keepdims=True) + NORM_EPS)
    vn = (vn * g_ref[...]).astype(BF16)
    for g in range(SGU_GROUPS):
        w = w_ref[g].astype(BF16)
        bias = bt_ref[:, g:g + 1]
        cs = slice(g * gdim, (g + 1) * gdim)
        for c in range(rows // SGU_CHUNK):
            rs = slice(c * SGU_CHUNK, (c + 1) * SGU_CHUNK)
            sp = jnp.dot(w, vn[rs, cs], preferred_element_type=F32) + bias
            o_ref[rs, cs] = (u_ref[rs, cs].astype(F32) * sp).astype(o_ref.dtype)


def _sgu(uv, sgu_norm, sgu_w, sgu_b):
    t = uv.shape[0]
    width = sgu_norm.shape[0]
    tr = 2 * SGU_CHUNK
    return pl.pallas_call(
        _sgu_kernel,
        grid=(t // tr,),
        in_specs=[
            pl.BlockSpec((tr, width), lambda i: (i, 0)),
            pl.BlockSpec((tr, width), lambda i: (i, 1)),
            pl.BlockSpec((1, width), lambda i: (0, 0)),
            pl.BlockSpec((SGU_GROUPS, SGU_CHUNK, SGU_CHUNK), lambda i: (0, 0, 0)),
            pl.BlockSpec((SGU_CHUNK, SGU_GROUPS), lambda i: (0, 0)),
        ],
        out_specs=pl.BlockSpec((tr, width), lambda i: (i, 0)),
        out_shape=jax.ShapeDtypeStruct((t, width), BF16),
        compiler_params=_params(),
        name="sgu",
    )(uv, uv, sgu_norm.reshape(1, width), sgu_w, sgu_b.T)


def _merge_kernel(at_ref, sg_ref, wa_ref, wb_ref, ga_ref, gb_ref, o_ref):
    a = jnp.dot(at_ref[...], wa_ref[...], preferred_element_type=F32)
    b = jnp.dot(sg_ref[...], wb_ref[...], preferred_element_type=F32)
    o_ref[...] = (ga_ref[...].astype(F32) * a + gb_ref[...].astype(F32) * b).astype(o_ref.dtype)


def _merge(attn, sgu, wa, wb, gates):
    t, d = attn.shape[0], wa.shape[1]
    tm, tn = 1024, 512
    return pl.pallas_call(
        _merge_kernel,
        grid=(t // tm, d // tn),
        in_specs=[
            pl.BlockSpec((tm, attn.shape[1]), lambda i, j: (i, 0)),
            pl.BlockSpec((tm, sgu.shape[1]), lambda i, j: (i, 0)),
            pl.BlockSpec((wa.shape[0], tn), lambda i, j: (0, j)),
            pl.BlockSpec((wb.shape[0], tn), lambda i, j: (0, j)),
            pl.BlockSpec((tm, tn), lambda i, j: (i, j)),
            pl.BlockSpec((tm, tn), lambda i, j: (i, d // tn + j)),
        ],
        out_specs=pl.BlockSpec((tm, tn), lambda i, j: (i, j)),
        out_shape=jax.ShapeDtypeStruct((t, d), BF16),
        compiler_params=_params(),
        name="branch_merge",
    )(attn, sgu, wa, wb, gates, gates)


def _out_proj_kernel(a_ref, w_ref, x_ref, g_ref, x1_ref, xg_ref, ss_ref):
    x1 = x_ref[...] + jnp.dot(a_ref[...], w_ref[...], preferred_element_type=F32)
    x1_ref[...] = x1
    xg_ref[...] = (x1 * g_ref[...]).astype(xg_ref.dtype)

    @pl.when(pl.program_id(1) == 0)
    def _():
        ss_ref[...] = jnp.zeros_like(ss_ref)

    ss_ref[...] += jnp.sum(x1 * x1, axis=-1, keepdims=True)


def _out_proj(a, w, x, g):
    t, k = a.shape
    d = w.shape[1]
    tm, tn = 1024, 512
    return pl.pallas_call(
        _out_proj_kernel,
        grid=(t // tm, d // tn),
        in_specs=[
            pl.BlockSpec((tm, k), lambda i, j: (i, 0)),
            pl.BlockSpec((k, tn), lambda i, j: (0, j)),
            pl.BlockSpec((tm, tn), lambda i, j: (i, j)),
            pl.BlockSpec((1, tn), lambda i, j: (0, j)),
        ],
        out_specs=[
            pl.BlockSpec((tm, tn), lambda i, j: (i, j)),
            pl.BlockSpec((tm, tn), lambda i, j: (i, j)),
            pl.BlockSpec((tm, LANES), lambda i, j: (i, 0)),
        ],
        out_shape=[
            jax.ShapeDtypeStruct((t, d), F32),
            jax.ShapeDtypeStruct((t, d), BF16),
            jax.ShapeDtypeStruct((t, LANES), F32),
        ],
        compiler_params=_params(),
        name="out_proj",
    )(a, w, x, g.reshape(1, d))


def _gate_up_kernel(xg_ref, ss_ref, wg_ref, wu_ref, o_ref, *, d_model):
    inv_rms = lax.rsqrt(ss_ref[:, 0:1] / d_model + NORM_EPS)
    xg = xg_ref[...]
    g = inv_rms * jnp.dot(xg, wg_ref[...], preferred_element_type=F32)
    u = inv_rms * jnp.dot(xg, wu_ref[...], preferred_element_type=F32)
    o_ref[...] = (jax.nn.silu(g) * u).astype(o_ref.dtype)


def _gate_up(xg, ss, wg, wu):
    t, k = xg.shape
    f = wg.shape[1]
    tm, tn = 2048, 256
    return pl.pallas_call(
        functools.partial(_gate_up_kernel, d_model=k),
        grid=(t // tm, f // tn),
        in_specs=[
            pl.BlockSpec((tm, k), lambda i, j: (i, 0)),
            pl.BlockSpec((tm, LANES), lambda i, j: (i, 0)),
            pl.BlockSpec((k, tn), lambda i, j: (0, j)),
            pl.BlockSpec((k, tn), lambda i, j: (0, j)),
        ],
        out_specs=pl.BlockSpec((tm, tn), lambda i, j: (i, j)),
        out_shape=jax.ShapeDtypeStruct((t, f), BF16),
        compiler_params=_params(),
        name="ffn_gate_up",
    )(xg, ss, wg, wu)


def _down_kernel(a_ref, w_ref, x_ref, o_ref):
    o_ref[...] = x_ref[...] + jnp.dot(a_ref[...], w_ref[...], preferred_element_type=F32)


def _down(a, w, x):
    t, k = a.shape
    d = w.shape[1]
    tm, tn = 512, 512
    return pl.pallas_call(
        _down_kernel,
        grid=(t // tm, d // tn),
        in_specs=[
            pl.BlockSpec((tm, k), lambda i, j: (i, 0)),
            pl.BlockSpec((k, tn), lambda i, j: (0, j)),
            pl.BlockSpec((tm, tn), lambda i, j: (i, j)),
        ],
        out_specs=pl.BlockSpec((tm, tn), lambda i, j: (i, j)),
        out_shape=jax.ShapeDtypeStruct((t, d), F32),
        compiler_params=_params(),
        name="ffn_down",
    )(a, w, x)


def _layer(x2, w, tables, batch, seq):
    d = x2.shape[1]
    sgu_width = w["sgu_norm"].shape[0]
    uv_first = 3 * N_GROUPS
    gates_first = uv_first + 2 * sgu_width // GROUP_WIDTH

    hs = _rmsnorm_orders(x2, w["attn_norm"], batch, seq)
    parts = []
    for g, (_, dil) in enumerate(DILATED_GROUPS):
        qkv = _proj(hs[dil], w["w_in"], g, N_GROUPS, 3, "qkv", f"proj_qkv{g}",
                    tables=tables[(seq, dil)], seq=seq)
        parts.append(_attn_group(qkv, seq, g))
    attn = _combine(parts, batch, seq)

    uv = _proj(hs[1], w["w_in"], uv_first, 1, 2 * sgu_width // GROUP_WIDTH, "gelu", "proj_uv")
    gates = _proj(hs[1], w["w_in"], gates_first, 1, 2 * d // GROUP_WIDTH, "sigmoid", "proj_gates")
    sgu = _sgu(uv, w["sgu_norm"], w["sgu_w"], w["sgu_b"])
    merged = _merge(attn, sgu, w["wb_a"], w["wb_b"], gates)
    x1, xg, ss = _out_proj(merged, w["w_out"], x2, w["ffn_norm"])

    gu = _gate_up(xg, ss, w["w_gate"], w["w_up"])
    return _down(gu, w["w_down"], x1)


def kernel(x_prompt, x_sample, attn_norm, w_in, sgu_norm, sgu_w, sgu_b, w_branch, w_out,
           ffn_norm, w_gate, w_up, w_down, final_norm):
    depth = w_in.shape[0]
    layers = [
        {
            "attn_norm": attn_norm[i], "w_in": w_in[i].astype(BF16),
            "sgu_norm": sgu_norm[i], "sgu_w": sgu_w[i], "sgu_b": sgu_b[i],
            "wb_a": w_branch[i, :GROUP_WIDTH].astype(BF16), "wb_b": w_branch[i, GROUP_WIDTH:].astype(BF16),
            "w_out": w_out[i].astype(BF16), "ffn_norm": ffn_norm[i],
            "w_gate": w_gate[i].astype(BF16), "w_up": w_up[i].astype(BF16),
            "w_down": w_down[i].astype(BF16),
        }
        for i in range(depth)
    ]
    seqs = sorted({x_prompt.shape[1], x_sample.shape[1]})
    tables = {(seq, dil): _rope_tables(seq, dil) for seq in seqs for dil in DILATIONS}
    outs = []
    for x in (x_prompt, x_sample):
        batch, seq, d = x.shape
        x2 = x.reshape(batch * seq, d)
        for w in layers:
            x2 = _layer(x2, w, tables, batch, seq)
        outs.append(_rmsnorm(x2, final_norm, F32, "final_rmsnorm").reshape(batch, seq, d))
    return tuple(outs)
```

```python
import functools
import math

import jax
import jax.numpy as jnp
from jax import lax
from jax.experimental import pallas as pl
from jax.experimental.pallas import tpu as pltpu

F32 = jnp.float32
BF16 = jnp.bfloat16

NORM_EPS = 1e-6
HEAD_DIM = 128
HEADS_PER_GROUP = 8
GROUP_WIDTH = HEADS_PER_GROUP * HEAD_DIM
DILATED_GROUPS = ((128, 1), (512, 4), (2048, 16))
N_GROUPS = len(DILATED_GROUPS)
DILATIONS = tuple(sorted({d for _, d in DILATED_GROUPS}))
ROPE_DIM = HEAD_DIM // 4
ROPE_HALF = ROPE_DIM // 2
ROPE_THETA = 500000.0
NEG_INF = -1e30
SGU_CHUNK = 128
SGU_GROUPS = 16
LANES = 128
BF16_SUBLANES = 16
EPILOGUE_CHUNK_ROWS = 256
V7X_VMEM_BYTES = 64 * 1024 * 1024
VMEM_LIMIT_BYTES = V7X_VMEM_BYTES - 8 * 1024 * 1024


def _params():
    return pltpu.CompilerParams(vmem_limit_bytes=VMEM_LIMIT_BYTES)


def _to_residue_major(rows, dil):
    n = rows // dil
    i = lax.broadcasted_iota(jnp.int32, (rows, rows), 0)
    j = lax.broadcasted_iota(jnp.int32, (rows, rows), 1)
    src = (i % n) * dil + i // n
    return jnp.where(j == src, 1.0, 0.0).astype(BF16)


def _to_natural(rows, dil):
    n = rows // dil
    i = lax.broadcasted_iota(jnp.int32, (rows, rows), 0)
    j = lax.broadcasted_iota(jnp.int32, (rows, rows), 1)
    src = (i % dil) * n + i // dil
    return jnp.where(j == src, 1.0, 0.0).astype(BF16)


def _rmsnorm_kernel(x_ref, g_ref, o_ref):
    x = x_ref[...]
    y = x * lax.rsqrt(jnp.mean(x * x, axis=-1, keepdims=True) + NORM_EPS)
    o_ref[...] = (y * g_ref[...]).astype(o_ref.dtype)


def _rmsnorm(x, g, out_dtype, name):
    t, d = x.shape
    tr = 256
    return pl.pallas_call(
        _rmsnorm_kernel,
        grid=(t // tr,),
        in_specs=[pl.BlockSpec((tr, d), lambda i: (i, 0)),
                  pl.BlockSpec((1, d), lambda i: (0, 0))],
        out_specs=pl.BlockSpec((tr, d), lambda i: (i, 0)),
        out_shape=jax.ShapeDtypeStruct((t, d), out_dtype),
        compiler_params=_params(),
        name=name,
    )(x, g.reshape(1, d))


def _rmsnorm_orders_kernel(x_ref, g_ref, *o_refs):
    rows = x_ref.shape[0]
    x = x_ref[...]
    y = x * lax.rsqrt(jnp.mean(x * x, axis=-1, keepdims=True) + NORM_EPS)
    y = (y * g_ref[...]).astype(BF16)
    for dil, o_ref in zip(DILATIONS, o_refs):
        if dil == 1:
            o_ref[...] = y
        else:
            n = rows // dil
            yp = jnp.dot(_to_residue_major(rows, dil), y, preferred_element_type=F32).astype(BF16)
            for r in range(dil):
                o_ref[r] = yp[r * n:(r + 1) * n, :]


def _rmsnorm_orders(x2, g, batch, seq):
    t, d = x2.shape
    tr = BF16_SUBLANES * max(DILATIONS)
    per_seq = seq // tr
    out_specs, out_shape = [], []
    for dil in DILATIONS:
        if dil == 1:
            out_specs.append(pl.BlockSpec((tr, d), lambda i: (i, 0)))
            out_shape.append(jax.ShapeDtypeStruct((t, d), BF16))
        else:
            out_specs.append(pl.BlockSpec((None, dil, tr // dil, d),
                                          lambda i: (i // per_seq, 0, i % per_seq, 0)))
            out_shape.append(jax.ShapeDtypeStruct((batch, dil, seq // dil, d), BF16))
    outs = pl.pallas_call(
        _rmsnorm_orders_kernel,
        grid=(t // tr,),
        in_specs=[pl.BlockSpec((tr, d), lambda i: (i, 0)),
                  pl.BlockSpec((1, d), lambda i: (0, 0))],
        out_specs=out_specs,
        out_shape=out_shape,
        compiler_params=_params(),
        name="attn_rmsnorm",
    )(x2, g.reshape(1, d))
    return {dil: o.reshape(t, d) for dil, o in zip(DILATIONS, outs)}


def _rope_table_kernel(f_ref, c_ref, sa_ref, sb_ref):
    rows = c_ref.shape[0]
    pos = (pl.program_id(0) * rows + lax.broadcasted_iota(jnp.int32, (rows, LANES), 0)).astype(F32)
    lane = lax.broadcasted_iota(jnp.int32, (rows, LANES), 1)
    ang = pos * f_ref[...]
    sin = jnp.sin(ang)
    c_ref[...] = jnp.cos(ang)
    sa_ref[...] = jnp.where(lane < ROPE_HALF, -sin, 0.0)
    sb_ref[...] = jnp.where((lane >= ROPE_HALF) & (lane < ROPE_DIM), sin, 0.0)


def _rope_reorder_kernel(*refs, plan):
    nat, outs = refs[:3], refs[3:]
    for k, (seq, dil) in enumerate(plan):
        n = seq // dil
        for r in range(dil):
            for src, dst in zip(nat, outs[3 * k:3 * k + 3]):
                dst[r * n:(r + 1) * n, :] = src[pl.ds(r, n, stride=dil), :]


def _rope_tables(seqs):
    inv_freq = ROPE_THETA ** (-jnp.arange(0, ROPE_DIM, 2, dtype=F32) / ROPE_DIM)
    lane_freq = jnp.concatenate([inv_freq, inv_freq, jnp.zeros((LANES - ROPE_DIM,), F32)]).reshape(1, LANES)
    s_max = max(seqs)
    rows = 256
    spec = pl.BlockSpec((rows, LANES), lambda i: (i, 0))
    shape = jax.ShapeDtypeStruct((s_max, LANES), F32)
    nat = pl.pallas_call(
        _rope_table_kernel,
        grid=(s_max // rows,),
        in_specs=[pl.BlockSpec((1, LANES), lambda i: (0, 0))],
        out_specs=[spec, spec, spec],
        out_shape=[shape, shape, shape],
        name="rope_tables",
    )(lane_freq)
    plan = [(seq, dil) for seq in seqs for dil in DILATIONS if dil != 1]
    reordered = pl.pallas_call(
        functools.partial(_rope_reorder_kernel, plan=plan),
        out_shape=[jax.ShapeDtypeStruct((seq, LANES), F32) for seq, _ in plan for _ in range(3)],
        compiler_params=_params(),
        name="rope_reorder",
    )(*nat)
    tables = {(seq, 1): tuple(nat) for seq in seqs}
    for k, key in enumerate(plan):
        tables[key] = tuple(reordered[3 * k:3 * k + 3])
    return tables


PROJ_CHUNK_ROWS = {"rope": EPILOGUE_CHUNK_ROWS, "plain": 1024, "gelu": EPILOGUE_CHUNK_ROWS,
                   "sigmoid": EPILOGUE_CHUNK_ROWS}


def _gelu(x):
    return 0.5 * x * (1.0 + lax.erf(x * math.sqrt(0.5)))


def _rider_specs(weights, grid):
    steps = grid[0] * grid[1]
    in_specs, out_specs, out_shape = [], [], []
    for w in weights:
        rows, cols = w.shape
        n = max(c for c in range(1, steps + 1) if rows % c == 0 and (rows // c) % BF16_SUBLANES == 0)
        imap = lambda i, j, n=n: (jnp.minimum(i * grid[1] + j, n - 1), 0)
        in_specs.append(pl.BlockSpec((rows // n, cols), imap))
        out_specs.append(pl.BlockSpec((rows // n, cols), imap))
        out_shape.append(jax.ShapeDtypeStruct((rows, cols), BF16))
    return in_specs, out_specs, out_shape


def _cast_riders(in_refs, out_refs):
    for src, dst in zip(in_refs, out_refs):
        dst[...] = src[...].astype(dst.dtype)


def _proj_kernel(*refs, kind, cm, n_riders):
    n_in = 2 + (3 if kind == "rope" else 0)
    a_ref, w_ref = refs[:2]
    o_ref = refs[n_in + n_riders]
    _cast_riders(refs[n_in:n_in + n_riders], refs[n_in + n_riders + 1:])
    for r0 in range(0, a_ref.shape[0], cm):
        rows = slice(r0, r0 + cm)
        acc = jnp.dot(a_ref[rows, :], w_ref[...], preferred_element_type=F32)
        if kind == "rope":
            c, sa, sb = (t_ref[rows, :] for t_ref in refs[2:n_in])
            for h in range(acc.shape[1] // HEAD_DIM):
                sl = slice(h * HEAD_DIM, (h + 1) * HEAD_DIM)
                z = acc[:, sl]
                rot = z * c + pltpu.roll(z, HEAD_DIM - ROPE_HALF, 1) * sa + pltpu.roll(z, ROPE_HALF, 1) * sb
                o_ref[rows, sl] = rot.astype(o_ref.dtype)
        elif kind == "gelu":
            o_ref[rows, :] = _gelu(acc).astype(o_ref.dtype)
        elif kind == "sigmoid":
            o_ref[rows, :] = jax.nn.sigmoid(acc).astype(o_ref.dtype)
        else:
            o_ref[rows, :] = acc.astype(o_ref.dtype)


def _proj(a, w, first, step, count, kind, name, tables=None, seq=None, riders=()):
    m, k = a.shape
    tm, tn = 1024, GROUP_WIDTH
    grid = (m // tm, count)
    in_specs = [pl.BlockSpec((tm, k), lambda i, j: (i, 0)),
                pl.BlockSpec((k, tn), lambda i, j: (0, first + step * j))]
    args = [a, w]
    if kind == "rope":
        per_seq = seq // tm
        tspec = pl.BlockSpec((tm, LANES), lambda i, j: (i % per_seq, 0))
        in_specs += [tspec, tspec, tspec]
        args += list(tables)
    r_in, r_out, r_shape = _rider_specs(riders, grid)
    out, *cast = pl.pallas_call(
        functools.partial(_proj_kernel, kind=kind, cm=PROJ_CHUNK_ROWS[kind], n_riders=len(riders)),
        grid=grid,
        in_specs=in_specs + r_in,
        out_specs=[pl.BlockSpec((tm, tn), lambda i, j: (i, j))] + r_out,
        out_shape=[jax.ShapeDtypeStruct((m, count * tn), BF16)] + r_shape,
        compiler_params=_params(),
        name=name,
    )(*args, *riders)
    return out, cast


def _attn_kernel(q_ref, k_ref, v_ref, o_ref, m_ref, l_ref, *, qs, win, sub_len, radius):
    tq = q_ref.shape[0]
    kv_rows = k_ref.shape[0]
    rel = (lax.broadcasted_iota(jnp.int32, (qs, win), 0) - lax.broadcasted_iota(jnp.int32, (qs, win), 1))
    lane = lax.broadcasted_iota(jnp.int32, (qs, LANES), 1)
    scale = 1.0 / math.sqrt(HEAD_DIM)
    for qb in range(tq // qs):
        rows = slice(qb * qs, (qb + 1) * qs)
        if kv_rows == sub_len:
            l0 = pl.program_id(1) * tq + qb * qs
            start = pl.multiple_of(jnp.clip(l0 - radius, 0, sub_len - win), radius)
            kv = pl.ds(start, win)
        else:
            l0 = (qb * qs) % sub_len
            start = min(max(l0 - radius, 0), sub_len - win)
            kv = pl.ds((qb * qs) // sub_len * sub_len + start, win)
        valid = jnp.abs(rel + (l0 - start)) <= radius
        m_all = jnp.zeros((qs, LANES), F32)
        l_all = jnp.zeros((qs, LANES), F32)
        for h in range(HEADS_PER_GROUP):
            sl = slice(h * HEAD_DIM, (h + 1) * HEAD_DIM)
            s = lax.dot_general(q_ref[rows, sl], k_ref[kv, sl], (((1,), (1,)), ((), ())),
                                preferred_element_type=F32) * scale
            s = jnp.where(valid, s, NEG_INF)
            m = jnp.max(s, axis=-1, keepdims=True)
            p = jnp.exp(s - m)
            l = jnp.sum(p, axis=-1, keepdims=True)
            o = jnp.dot(p.astype(BF16), v_ref[kv, sl], preferred_element_type=F32)
            o_ref[rows, sl] = o.astype(o_ref.dtype)
            m_all = jnp.where(lane == h, m, m_all)
            l_all = jnp.where(lane == h, l, l_all)
        m_ref[rows, :] = m_all
        l_ref[rows, :] = l_all


def _attn_group(qk, v, seq, g):
    t = qk.shape[0]
    window, dil = DILATED_GROUPS[g]
    radius = window // (2 * dil)
    sub_len = seq // dil
    tq = 256
    qs = min(128, sub_len)
    win = min(qs + 2 * radius, sub_len)
    if sub_len >= tq:
        per_sub = sub_len // tq
        grid = (t // sub_len, per_sub)
        kv_rows = sub_len
        q_map = lambda s, j: (s * per_sub + j, 0)
        k_map = lambda s, j: (s, 1)
        v_map = lambda s, j: (s, 0)
    else:
        grid = (t // tq, 1)
        kv_rows = tq
        q_map = lambda s, j: (s, 0)
        k_map = lambda s, j: (s, 1)
        v_map = lambda s, j: (s, 0)
    return pl.pallas_call(
        functools.partial(_attn_kernel, qs=qs, win=win, sub_len=sub_len, radius=radius),
        grid=grid,
        in_specs=[
            pl.BlockSpec((tq, GROUP_WIDTH), q_map),
            pl.BlockSpec((kv_rows, GROUP_WIDTH), k_map),
            pl.BlockSpec((kv_rows, GROUP_WIDTH), v_map),
        ],
        out_specs=[
            pl.BlockSpec((tq, GROUP_WIDTH), q_map),
            pl.BlockSpec((tq, LANES), q_map),
            pl.BlockSpec((tq, LANES), q_map),
        ],
        out_shape=[
            jax.ShapeDtypeStruct((t, GROUP_WIDTH), BF16),
            jax.ShapeDtypeStruct((t, LANES), F32),
            jax.ShapeDtypeStruct((t, LANES), F32),
        ],
        compiler_params=_params(),
        name=f"attn_g{g}",
    )(qk, qk, v)


def _combine_kernel(*refs):
    o_refs, m_refs, l_refs = refs[0:N_GROUPS], refs[N_GROUPS:2 * N_GROUPS], refs[2 * N_GROUPS:3 * N_GROUPS]
    out_ref, m_scr, l_scr = refs[3 * N_GROUPS:]
    rows = out_ref.shape[0]
    os_, ms, ls = [], [], []
    for g, (_, dil) in enumerate(DILATED_GROUPS):
        if dil == 1:
            os_.append(o_refs[g][...].astype(F32))
            ms.append(m_refs[g][...])
            ls.append(l_refs[g][...])
        else:
            n = rows // dil
            for r in range(dil):
                m_scr[g, pl.ds(r, n, stride=dil), :] = m_refs[g][r]
                l_scr[g, pl.ds(r, n, stride=dil), :] = l_refs[g][r]
            ms.append(m_scr[g])
            ls.append(l_scr[g])
            og = o_refs[g][...].reshape(rows, GROUP_WIDTH)
            os_.append(jnp.dot(_to_natural(rows, dil), og, preferred_element_type=F32))
    m_max = functools.reduce(jnp.maximum, ms)
    ws = [jnp.exp(m - m_max) for m in ms]
    den = sum(w * l for w, l in zip(ws, ls))
    for h in range(HEADS_PER_GROUP):
        sl = slice(h * HEAD_DIM, (h + 1) * HEAD_DIM)
        num = sum(w[:, h:h + 1] * o[:, sl] for w, o in zip(ws, os_))
        out_ref[:, sl] = (num / den[:, h:h + 1]).astype(out_ref.dtype)


def _combine(parts, batch, seq):
    t = batch * seq
    tr = BF16_SUBLANES * 2 * max(DILATIONS)
    per_seq = seq // tr
    o_specs, s_specs, o_args, m_args, l_args = [], [], [], [], []
    for (o, m, l), (_, dil) in zip(parts, DILATED_GROUPS):
        if dil == 1:
            o_specs.append(pl.BlockSpec((tr, GROUP_WIDTH), lambda i: (i, 0)))
            s_specs.append(pl.BlockSpec((tr, LANES), lambda i: (i, 0)))
            o_args.append(o), m_args.append(m), l_args.append(l)
        else:
            imap = lambda i: (i // per_seq, 0, i % per_seq, 0)
            o_specs.append(pl.BlockSpec((None, dil, tr // dil, GROUP_WIDTH), imap))
            s_specs.append(pl.BlockSpec((None, dil, tr // dil, LANES), imap))
            o_args.append(o.reshape(batch, dil, seq // dil, GROUP_WIDTH))
            m_args.append(m.reshape(batch, dil, seq // dil, LANES))
            l_args.append(l.reshape(batch, dil, seq // dil, LANES))
    return pl.pallas_call(
        _combine_kernel,
        grid=(t // tr,),
        in_specs=o_specs + s_specs + s_specs,
        out_specs=pl.BlockSpec((tr, GROUP_WIDTH), lambda i: (i, 0)),
        out_shape=jax.ShapeDtypeStruct((t, GROUP_WIDTH), BF16),
        scratch_shapes=[pltpu.VMEM((N_GROUPS, tr, LANES), F32), pltpu.VMEM((N_GROUPS, tr, LANES), F32)],
        compiler_params=_params(),
        name="attn_combine",
    )(*o_args, *m_args, *l_args)


def _sgu_kernel(u_ref, vs_ref, g_ref, w_ref, bt_ref, o_ref):
    rows, width = vs_ref.shape
    gdim = width // SGU_GROUPS
    gv = vs_ref[...].astype(F32)
    vn = gv * lax.rsqrt(jnp.mean(gv * gv, axis=-1, keepdims=True) + NORM_EPS)
    vn = (vn * g_ref[...]).astype(BF16)
    for g in range(SGU_GROUPS):
        w = w_ref[g].astype(BF16)
        bias = bt_ref[:, g:g + 1]
        cs = slice(g * gdim, (g + 1) * gdim)
        for c in range(rows // SGU_CHUNK):
            rs = slice(c * SGU_CHUNK, (c + 1) * SGU_CHUNK)
            sp = jnp.dot(w, vn[rs, cs], preferred_element_type=F32) + bias
            o_ref[rs, cs] = (u_ref[rs, cs].astype(F32) * sp).astype(o_ref.dtype)


def _sgu(uv, sgu_norm, sgu_w, sgu_b):
    t = uv.shape[0]
    width = sgu_norm.shape[0]
    tr = 4 * SGU_CHUNK
    return pl.pallas_call(
        _sgu_kernel,
        grid=(t // tr,),
        in_specs=[
            pl.BlockSpec((tr, width), lambda i: (i, 0)),
            pl.BlockSpec((tr, width), lambda i: (i, 1)),
            pl.BlockSpec((1, width), lambda i: (0, 0)),
            pl.BlockSpec((SGU_GROUPS, SGU_CHUNK, SGU_CHUNK), lambda i: (0, 0, 0)),
            pl.BlockSpec((SGU_CHUNK, SGU_GROUPS), lambda i: (0, 0)),
        ],
        out_specs=pl.BlockSpec((tr, width), lambda i: (i, 0)),
        out_shape=jax.ShapeDtypeStruct((t, width), BF16),
        compiler_params=_params(),
        name="sgu",
    )(uv, uv, sgu_norm.reshape(1, width), sgu_w, sgu_b.T)


def _merge_kernel(at_ref, sg_ref, w_ref, ga_ref, gb_ref, *rest, n_riders):
    o_ref = rest[n_riders]
    _cast_riders(rest[:n_riders], rest[n_riders + 1:])
    ka = at_ref.shape[1]
    for r0 in range(0, at_ref.shape[0], EPILOGUE_CHUNK_ROWS):
        rows = slice(r0, r0 + EPILOGUE_CHUNK_ROWS)
        a = jnp.dot(at_ref[rows, :], w_ref[:ka, :], preferred_element_type=F32)
        b = jnp.dot(sg_ref[rows, :], w_ref[ka:, :], preferred_element_type=F32)
        o_ref[rows, :] = (ga_ref[rows, :].astype(F32) * a + gb_ref[rows, :].astype(F32) * b).astype(o_ref.dtype)


def _merge(attn, sgu, w_branch, gates, riders=()):
    t, d = attn.shape[0], w_branch.shape[1]
    tm, tn = 1024, 512
    grid = (t // tm, d // tn)
    r_in, r_out, r_shape = _rider_specs(riders, grid)
    out, *cast = pl.pallas_call(
        functools.partial(_merge_kernel, n_riders=len(riders)),
        grid=grid,
        in_specs=[
            pl.BlockSpec((tm, attn.shape[1]), lambda i, j: (i, 0)),
            pl.BlockSpec((tm, sgu.shape[1]), lambda i, j: (i, 0)),
            pl.BlockSpec((w_branch.shape[0], tn), lambda i, j: (0, j)),
            pl.BlockSpec((tm, tn), lambda i, j: (i, j)),
            pl.BlockSpec((tm, tn), lambda i, j: (i, d // tn + j)),
        ] + r_in,
        out_specs=[pl.BlockSpec((tm, tn), lambda i, j: (i, j))] + r_out,
        out_shape=[jax.ShapeDtypeStruct((t, d), BF16)] + r_shape,
        compiler_params=_params(),
        name="branch_merge",
    )(attn, sgu, w_branch, gates, gates, *riders)
    return out, cast


def _out_proj_kernel(a_ref, w_ref, x_ref, g_ref, x1_ref, xg_ref, ss_ref):
    @pl.when(pl.program_id(1) == 0)
    def _():
        ss_ref[...] = jnp.zeros_like(ss_ref)

    for r0 in range(0, a_ref.shape[0], EPILOGUE_CHUNK_ROWS):
        rows = slice(r0, r0 + EPILOGUE_CHUNK_ROWS)
        x1 = x_ref[rows, :] + jnp.dot(a_ref[rows, :], w_ref[...], preferred_element_type=F32)
        x1_ref[rows, :] = x1
        xg_ref[rows, :] = (x1 * g_ref[...]).astype(xg_ref.dtype)
        ss_ref[rows, :] += jnp.sum(x1 * x1, axis=-1, keepdims=True)


def _out_proj(a, w, x, g):
    t, k = a.shape
    d = w.shape[1]
    tm, tn = 1024, 512
    return pl.pallas_call(
        _out_proj_kernel,
        grid=(t // tm, d // tn),
        in_specs=[
            pl.BlockSpec((tm, k), lambda i, j: (i, 0)),
            pl.BlockSpec((k, tn), lambda i, j: (0, j)),
            pl.BlockSpec((tm, tn), lambda i, j: (i, j)),
            pl.BlockSpec((1, tn), lambda i, j: (0, j)),
        ],
        out_specs=[
            pl.BlockSpec((tm, tn), lambda i, j: (i, j)),
            pl.BlockSpec((tm, tn), lambda i, j: (i, j)),
            pl.BlockSpec((tm, LANES), lambda i, j: (i, 0)),
        ],
        out_shape=[
            jax.ShapeDtypeStruct((t, d), F32),
            jax.ShapeDtypeStruct((t, d), BF16),
            jax.ShapeDtypeStruct((t, LANES), F32),
        ],
        compiler_params=_params(),
        name="out_proj",
    )(a, w, x, g.reshape(1, d))


def _gate_up_kernel(xg_ref, ss_ref, wg_ref, wu_ref, o_ref, *, d_model):
    for r0 in range(0, xg_ref.shape[0], EPILOGUE_CHUNK_ROWS):
        rows = slice(r0, r0 + EPILOGUE_CHUNK_ROWS)
        inv_rms = lax.rsqrt(ss_ref[rows, 0:1] / d_model + NORM_EPS)
        xg = xg_ref[rows, :]
        g = inv_rms * jnp.dot(xg, wg_ref[...], preferred_element_type=F32)
        u = inv_rms * jnp.dot(xg, wu_ref[...], preferred_element_type=F32)
        o_ref[rows, :] = (jax.nn.silu(g) * u).astype(o_ref.dtype)


def _gate_up(xg, ss, wg, wu):
    t, k = xg.shape
    f = wg.shape[1]
    tm, tn = 2048, 256
    return pl.pallas_call(
        functools.partial(_gate_up_kernel, d_model=k),
        grid=(t // tm, f // tn),
        in_specs=[
            pl.BlockSpec((tm, k), lambda i, j: (i, 0)),
            pl.BlockSpec((tm, LANES), lambda i, j: (i, 0)),
            pl.BlockSpec((k, tn), lambda i, j: (0, j)),
            pl.BlockSpec((k, tn), lambda i, j: (0, j)),
        ],
        out_specs=pl.BlockSpec((tm, tn), lambda i, j: (i, j)),
        out_shape=jax.ShapeDtypeStruct((t, f), BF16),
        compiler_params=_params(),
        name="ffn_gate_up",
    )(xg, ss, wg, wu)


def _down_kernel(a_ref, w_ref, x_ref, o_ref):
    o_ref[...] = x_ref[...] + jnp.dot(a_ref[...], w_ref[...], preferred_element_type=F32)


def _down(a, w, x):
    t, k = a.shape
    d = w.shape[1]
    tm, tn = 512, 512
    return pl.pallas_call(
        _down_kernel,
        grid=(t // tm, d // tn),
        in_specs=[
            pl.BlockSpec((tm, k), lambda i, j: (i, 0)),
            pl.BlockSpec((k, tn), lambda i, j: (0, j)),
            pl.BlockSpec((tm, tn), lambda i, j: (i, j)),
        ],
        out_specs=pl.BlockSpec((tm, tn), lambda i, j: (i, j)),
        out_shape=jax.ShapeDtypeStruct((t, d), F32),
        compiler_params=_params(),
        name="ffn_down",
    )(a, w, x)


def _layer(x2, w, mxu_w, tables, batch, seq):
    d = x2.shape[1]
    sgu_width = w["sgu_norm"].shape[0]
    uv_first = 3 * N_GROUPS
    gates_first = uv_first + 2 * sgu_width // GROUP_WIDTH
    cast = mxu_w is None

    hs = _rmsnorm_orders(x2, w["attn_norm"], batch, seq)
    parts = []
    for g, (_, dil) in enumerate(DILATED_GROUPS):
        qk, _ = _proj(hs[dil], w["w_in"], g, N_GROUPS, 2, "rope", f"proj_qk{g}",
                      tables=tables[(seq, dil)], seq=seq)
        v, _ = _proj(hs[dil], w["w_in"], 2 * N_GROUPS + g, 1, 1, "plain", f"proj_v{g}")
        parts.append(_attn_group(qk, v, seq, g))
    attn = _combine(parts, batch, seq)

    uv, c_uv = _proj(hs[1], w["w_in"], uv_first, 1, 2 * sgu_width // GROUP_WIDTH, "gelu", "proj_uv",
                     riders=(w["w_gate"], w["w_out"]) if cast else ())
    gates, c_gates = _proj(hs[1], w["w_in"], gates_first, 1, 2 * d // GROUP_WIDTH, "sigmoid", "proj_gates",
                           riders=(w["w_up"], w["w_branch"]) if cast else ())
    if cast:
        mxu_w = {"w_gate": c_uv[0], "w_out": c_uv[1], "w_up": c_gates[0], "w_branch": c_gates[1]}
    sgu = _sgu(uv, w["sgu_norm"], w["sgu_w"], w["sgu_b"])
    merged, c_merge = _merge(attn, sgu, mxu_w["w_branch"], gates, riders=(w["w_down"],) if cast else ())
    if cast:
        mxu_w["w_down"] = c_merge[0]
    x1, xg, ss = _out_proj(merged, mxu_w["w_out"], x2, w["ffn_norm"])

    gu = _gate_up(xg, ss, mxu_w["w_gate"], mxu_w["w_up"])
    return _down(gu, mxu_w["w_down"], x1), mxu_w


def kernel(x_prompt, x_sample, attn_norm, w_in, sgu_norm, sgu_w, sgu_b, w_branch, w_out,
           ffn_norm, w_gate, w_up, w_down, final_norm):
    depth = w_in.shape[0]
    layers = [
        {
            "attn_norm": attn_norm[i], "w_in": w_in[i].astype(BF16),
            "sgu_norm": sgu_norm[i], "sgu_w": sgu_w[i], "sgu_b": sgu_b[i],
            "w_branch": w_branch[i], "w_out": w_out[i], "ffn_norm": ffn_norm[i],
            "w_gate": w_gate[i], "w_up": w_up[i], "w_down": w_down[i],
        }
        for i in range(depth)
    ]
    mxu_weights = [None] * depth
    tables = _rope_tables(sorted({x_prompt.shape[1], x_sample.shape[1]}))
    outs = []
    for x in (x_prompt, x_sample):
        batch, seq, d = x.shape
        x2 = x.reshape(batch * seq, d)
        for i, w in enumerate(layers):
            x2, mxu_weights[i] = _layer(x2, w, mxu_weights[i], tables, batch, seq)
        outs.append(_rmsnorm(x2, final_norm, F32, "final_rmsnorm").reshape(batch, seq, d))
    return tuple(outs)
```

```python
import functools
import math

import jax
import jax.numpy as jnp
from jax import lax
from jax.experimental import pallas as pl
from jax.experimental.pallas import tpu as pltpu

F32 = jnp.float32
BF16 = jnp.bfloat16

NORM_EPS = 1e-6
HEAD_DIM = 128
HEADS_PER_GROUP = 8
GROUP_WIDTH = HEADS_PER_GROUP * HEAD_DIM
DILATED_GROUPS = ((128, 1), (512, 4), (2048, 16))
N_GROUPS = len(DILATED_GROUPS)
DILATIONS = tuple(sorted({d for _, d in DILATED_GROUPS}))
ROPE_DIM = HEAD_DIM // 4
ROPE_HALF = ROPE_DIM // 2
ROPE_THETA = 500000.0
NEG_INF = -1e30
LOG2_E = math.log2(math.e)
SGU_CHUNK = 128
SGU_GROUPS = 16
LANES = 128
BF16_SUBLANES = 16
EPILOGUE_CHUNK_ROWS = 256
V7X_VMEM_BYTES = 64 * 1024 * 1024
VMEM_LIMIT_BYTES = V7X_VMEM_BYTES - 8 * 1024 * 1024


def _params():
    return pltpu.CompilerParams(vmem_limit_bytes=VMEM_LIMIT_BYTES)


def _to_residue_major(rows, dil):
    n = rows // dil
    i = lax.broadcasted_iota(jnp.int32, (rows, rows), 0)
    j = lax.broadcasted_iota(jnp.int32, (rows, rows), 1)
    src = (i % n) * dil + i // n
    return jnp.where(j == src, 1.0, 0.0).astype(BF16)


def _to_natural(rows, dil):
    n = rows // dil
    i = lax.broadcasted_iota(jnp.int32, (rows, rows), 0)
    j = lax.broadcasted_iota(jnp.int32, (rows, rows), 1)
    src = (i % dil) * n + i // dil
    return jnp.where(j == src, 1.0, 0.0).astype(BF16)


def _rmsnorm_kernel(x_ref, g_ref, o_ref):
    x = x_ref[...]
    y = x * lax.rsqrt(jnp.mean(x * x, axis=-1, keepdims=True) + NORM_EPS)
    o_ref[...] = (y * g_ref[...]).astype(o_ref.dtype)


def _rmsnorm(x, g, out_dtype, name):
    t, d = x.shape
    tr = 256
    return pl.pallas_call(
        _rmsnorm_kernel,
        grid=(t // tr,),
        in_specs=[pl.BlockSpec((tr, d), lambda i: (i, 0)),
                  pl.BlockSpec((1, d), lambda i: (0, 0))],
        out_specs=pl.BlockSpec((tr, d), lambda i: (i, 0)),
        out_shape=jax.ShapeDtypeStruct((t, d), out_dtype),
        compiler_params=_params(),
        name=name,
    )(x, g.reshape(1, d))


def _rmsnorm_orders_kernel(x_ref, g_ref, *o_refs):
    rows = x_ref.shape[0]
    x = x_ref[...]
    y = x * lax.rsqrt(jnp.mean(x * x, axis=-1, keepdims=True) + NORM_EPS)
    y = (y * g_ref[...]).astype(BF16)
    for dil, o_ref in zip(DILATIONS, o_refs):
        if dil == 1:
            o_ref[...] = y
        else:
            n = rows // dil
            yp = jnp.dot(_to_residue_major(rows, dil), y, preferred_element_type=F32).astype(BF16)
            for r in range(dil):
                o_ref[r] = yp[r * n:(r + 1) * n, :]


def _rmsnorm_orders(x2, g, batch, seq):
    t, d = x2.shape
    tr = BF16_SUBLANES * max(DILATIONS)
    per_seq = seq // tr
    out_specs, out_shape = [], []
    for dil in DILATIONS:
        if dil == 1:
            out_specs.append(pl.BlockSpec((tr, d), lambda i: (i, 0)))
            out_shape.append(jax.ShapeDtypeStruct((t, d), BF16))
        else:
            out_specs.append(pl.BlockSpec((None, dil, tr // dil, d),
                                          lambda i: (i // per_seq, 0, i % per_seq, 0)))
            out_shape.append(jax.ShapeDtypeStruct((batch, dil, seq // dil, d), BF16))
    outs = pl.pallas_call(
        _rmsnorm_orders_kernel,
        grid=(t // tr,),
        in_specs=[pl.BlockSpec((tr, d), lambda i: (i, 0)),
                  pl.BlockSpec((1, d), lambda i: (0, 0))],
        out_specs=out_specs,
        out_shape=out_shape,
        compiler_params=_params(),
        name="attn_rmsnorm",
    )(x2, g.reshape(1, d))
    return {dil: o.reshape(t, d) for dil, o in zip(DILATIONS, outs)}


def _rope_table_kernel(f_ref, c_ref, sa_ref, sb_ref):
    rows = c_ref.shape[0]
    pos = (pl.program_id(0) * rows + lax.broadcasted_iota(jnp.int32, (rows, LANES), 0)).astype(F32)
    lane = lax.broadcasted_iota(jnp.int32, (rows, LANES), 1)
    ang = pos * f_ref[...]
    sin = jnp.sin(ang)
    c_ref[...] = jnp.cos(ang)
    sa_ref[...] = jnp.where(lane < ROPE_HALF, -sin, 0.0)
    sb_ref[...] = jnp.where((lane >= ROPE_HALF) & (lane < ROPE_DIM), sin, 0.0)


def _rope_reorder_kernel(*refs, plan):
    nat, outs = refs[:3], refs[3:]
    for k, (seq, dil) in enumerate(plan):
        n = seq // dil
        for r in range(dil):
            for src, dst in zip(nat, outs[3 * k:3 * k + 3]):
                dst[r * n:(r + 1) * n, :] = src[pl.ds(r, n, stride=dil), :]


def _rope_tables(seqs):
    inv_freq = ROPE_THETA ** (-jnp.arange(0, ROPE_DIM, 2, dtype=F32) / ROPE_DIM)
    lane_freq = jnp.concatenate([inv_freq, inv_freq, jnp.zeros((LANES - ROPE_DIM,), F32)]).reshape(1, LANES)
    s_max = max(seqs)
    rows = 256
    spec = pl.BlockSpec((rows, LANES), lambda i: (i, 0))
    shape = jax.ShapeDtypeStruct((s_max, LANES), F32)
    nat = pl.pallas_call(
        _rope_table_kernel,
        grid=(s_max // rows,),
        in_specs=[pl.BlockSpec((1, LANES), lambda i: (0, 0))],
        out_specs=[spec, spec, spec],
        out_shape=[shape, shape, shape],
        name="rope_tables",
    )(lane_freq)
    plan = [(seq, dil) for seq in seqs for dil in DILATIONS if dil != 1]
    reordered = pl.pallas_call(
        functools.partial(_rope_reorder_kernel, plan=plan),
        out_shape=[jax.ShapeDtypeStruct((seq, LANES), F32) for seq, _ in plan for _ in range(3)],
        compiler_params=_params(),
        name="rope_reorder",
    )(*nat)
    tables = {(seq, 1): tuple(nat) for seq in seqs}
    for k, key in enumerate(plan):
        tables[key] = tuple(reordered[3 * k:3 * k + 3])
    return tables


PROJ_CHUNK_ROWS = {"rope": EPILOGUE_CHUNK_ROWS, "plain": 1024, "gelu": EPILOGUE_CHUNK_ROWS,
                   "sigmoid": EPILOGUE_CHUNK_ROWS}


def _gelu(x):
    return 0.5 * x * (1.0 + lax.erf(x * math.sqrt(0.5)))


def _rider_specs(weights, grid):
    steps = grid[0] * grid[1]
    in_specs, out_specs, out_shape = [], [], []
    for w in weights:
        rows, cols = w.shape
        n = max(c for c in range(1, steps + 1) if rows % c == 0 and (rows // c) % BF16_SUBLANES == 0)
        imap = lambda i, j, n=n: (jnp.minimum(i * grid[1] + j, n - 1), 0)
        in_specs.append(pl.BlockSpec((rows // n, cols), imap))
        out_specs.append(pl.BlockSpec((rows // n, cols), imap))
        out_shape.append(jax.ShapeDtypeStruct((rows, cols), BF16))
    return in_specs, out_specs, out_shape


def _cast_riders(in_refs, out_refs):
    for src, dst in zip(in_refs, out_refs):
        dst[...] = src[...].astype(dst.dtype)


def _proj_kernel(*refs, kind, cm, n_riders):
    n_in = 2 + (3 if kind == "rope" else 0)
    a_ref, w_ref = refs[:2]
    o_ref = refs[n_in + n_riders]
    _cast_riders(refs[n_in:n_in + n_riders], refs[n_in + n_riders + 1:])
    for r0 in range(0, a_ref.shape[0], cm):
        rows = slice(r0, r0 + cm)
        acc = jnp.dot(a_ref[rows, :], w_ref[...], preferred_element_type=F32)
        if kind == "rope":
            c, sa, sb = (t_ref[rows, :] for t_ref in refs[2:n_in])
            for h in range(acc.shape[1] // HEAD_DIM):
                sl = slice(h * HEAD_DIM, (h + 1) * HEAD_DIM)
                z = acc[:, sl]
                rot = z * c + pltpu.roll(z, HEAD_DIM - ROPE_HALF, 1) * sa + pltpu.roll(z, ROPE_HALF, 1) * sb
                o_ref[rows, sl] = rot.astype(o_ref.dtype)
        elif kind == "gelu":
            o_ref[rows, :] = _gelu(acc).astype(o_ref.dtype)
        elif kind == "sigmoid":
            o_ref[rows, :] = jax.nn.sigmoid(acc).astype(o_ref.dtype)
        else:
            o_ref[rows, :] = acc.astype(o_ref.dtype)


def _proj(a, w, first, step, count, kind, name, tables=None, seq=None, riders=()):
    m, k = a.shape
    tm, tn = 1024, GROUP_WIDTH
    grid = (m // tm, count)
    in_specs = [pl.BlockSpec((tm, k), lambda i, j: (i, 0)),
                pl.BlockSpec((k, tn), lambda i, j: (0, first + step * j))]
    args = [a, w]
    if kind == "rope":
        per_seq = seq // tm
        tspec = pl.BlockSpec((tm, LANES), lambda i, j: (i % per_seq, 0))
        in_specs += [tspec, tspec, tspec]
        args += list(tables)
    r_in, r_out, r_shape = _rider_specs(riders, grid)
    out, *cast = pl.pallas_call(
        functools.partial(_proj_kernel, kind=kind, cm=PROJ_CHUNK_ROWS[kind], n_riders=len(riders)),
        grid=grid,
        in_specs=in_specs + r_in,
        out_specs=[pl.BlockSpec((tm, tn), lambda i, j: (i, j))] + r_out,
        out_shape=[jax.ShapeDtypeStruct((m, count * tn), BF16)] + r_shape,
        compiler_params=_params(),
        name=name,
    )(*args, *riders)
    return out, cast


def _attn_kernel(q_ref, k_ref, v_ref, o_ref, m_ref, l_ref, *, qs, win, sub_len, radius):
    tq = q_ref.shape[0]
    kv_rows = k_ref.shape[0]
    rel = (lax.broadcasted_iota(jnp.int32, (qs, win), 0) - lax.broadcasted_iota(jnp.int32, (qs, win), 1))
    scale = 1.0 / math.sqrt(HEAD_DIM)
    head_slices = [slice(h * HEAD_DIM, (h + 1) * HEAD_DIM) for h in range(HEADS_PER_GROUP)]
    m_ref[...] = jnp.zeros_like(m_ref)
    l_ref[...] = jnp.zeros_like(l_ref)
    for qb in range(tq // qs):
        rows = slice(qb * qs, (qb + 1) * qs)
        if kv_rows == sub_len:
            l0 = pl.program_id(1) * tq + qb * qs
            start = pl.multiple_of(jnp.clip(l0 - radius, 0, sub_len - win), radius)
            kv = pl.ds(start, win)
        else:
            l0 = (qb * qs) % sub_len
            start = min(max(l0 - radius, 0), sub_len - win)
            kv = pl.ds((qb * qs) // sub_len * sub_len + start, win)
        scores = [lax.dot_general(q_ref[rows, sl], k_ref[kv, sl], (((1,), (1,)), ((), ())),
                                  preferred_element_type=F32) for sl in head_slices]
        valid = jnp.abs(rel + (l0 - start)) <= radius
        for h, sl in enumerate(head_slices):
            s = jnp.where(valid, scores[h], NEG_INF)
            m = jnp.max(s, axis=-1, keepdims=True)
            p = jnp.exp2((s - m) * (scale * LOG2_E))
            l = jnp.sum(p, axis=-1, keepdims=True)
            o = jnp.dot(p.astype(BF16), v_ref[kv, sl], preferred_element_type=F32)
            o_ref[rows, sl] = o.astype(o_ref.dtype)
            m_ref[rows, h:h + 1] = m * scale
            l_ref[rows, h:h + 1] = l


def _attn_group(qk, v, seq, g):
    t = qk.shape[0]
    window, dil = DILATED_GROUPS[g]
    radius = window // (2 * dil)
    sub_len = seq // dil
    tq = 256
    qs = min(128, sub_len)
    win = min(qs + 2 * radius, sub_len)
    if sub_len >= tq:
        per_sub = sub_len // tq
        grid = (t // sub_len, per_sub)
        kv_rows = sub_len
        q_map = lambda s, j: (s * per_sub + j, 0)
        k_map = lambda s, j: (s, 1)
        v_map = lambda s, j: (s, 0)
    else:
        grid = (t // tq, 1)
        kv_rows = tq
        q_map = lambda s, j: (s, 0)
        k_map = lambda s, j: (s, 1)
        v_map = lambda s, j: (s, 0)
    return pl.pallas_call(
        functools.partial(_attn_kernel, qs=qs, win=win, sub_len=sub_len, radius=radius),
        grid=grid,
        in_specs=[
            pl.BlockSpec((tq, GROUP_WIDTH), q_map),
            pl.BlockSpec((kv_rows, GROUP_WIDTH), k_map),
            pl.BlockSpec((kv_rows, GROUP_WIDTH), v_map),
        ],
        out_specs=[
            pl.BlockSpec((tq, GROUP_WIDTH), q_map),
            pl.BlockSpec((tq, LANES), q_map),
            pl.BlockSpec((tq, LANES), q_map),
        ],
        out_shape=[
            jax.ShapeDtypeStruct((t, GROUP_WIDTH), BF16),
            jax.ShapeDtypeStruct((t, LANES), F32),
            jax.ShapeDtypeStruct((t, LANES), F32),
        ],
        compiler_params=_params(),
        name=f"attn_g{g}",
    )(qk, qk, v)


def _combine_kernel(*refs):
    o_refs, m_refs, l_refs = refs[0:N_GROUPS], refs[N_GROUPS:2 * N_GROUPS], refs[2 * N_GROUPS:3 * N_GROUPS]
    out_ref, m_scr, l_scr = refs[3 * N_GROUPS:]
    rows = out_ref.shape[0]
    os_, ms, ls = [], [], []
    for g, (_, dil) in enumerate(DILATED_GROUPS):
        if dil == 1:
            os_.append(o_refs[g][...].astype(F32))
            ms.append(m_refs[g][...])
            ls.append(l_refs[g][...])
        else:
            n = rows // dil
            for r in range(dil):
                m_scr[g, pl.ds(r, n, stride=dil), :] = m_refs[g][r]
                l_scr[g, pl.ds(r, n, stride=dil), :] = l_refs[g][r]
            ms.append(m_scr[g])
            ls.append(l_scr[g])
            og = o_refs[g][...].reshape(rows, GROUP_WIDTH)
            os_.append(jnp.dot(_to_natural(rows, dil), og, preferred_element_type=F32))
    m_max = functools.reduce(jnp.maximum, ms)
    ws = [jnp.exp(m - m_max) for m in ms]
    den = sum(w * l for w, l in zip(ws, ls))
    for h in range(HEADS_PER_GROUP):
        sl = slice(h * HEAD_DIM, (h + 1) * HEAD_DIM)
        num = sum(w[:, h:h + 1] * o[:, sl] for w, o in zip(ws, os_))
        out_ref[:, sl] = (num / den[:, h:h + 1]).astype(out_ref.dtype)


def _combine(parts, batch, seq):
    t = batch * seq
    tr = BF16_SUBLANES * 2 * max(DILATIONS)
    per_seq = seq // tr
    o_specs, s_specs, o_args, m_args, l_args = [], [], [], [], []
    for (o, m, l), (_, dil) in zip(parts, DILATED_GROUPS):
        if dil == 1:
            o_specs.append(pl.BlockSpec((tr, GROUP_WIDTH), lambda i: (i, 0)))
            s_specs.append(pl.BlockSpec((tr, LANES), lambda i: (i, 0)))
            o_args.append(o), m_args.append(m), l_args.append(l)
        else:
            imap = lambda i: (i // per_seq, 0, i % per_seq, 0)
            o_specs.append(pl.BlockSpec((None, dil, tr // dil, GROUP_WIDTH), imap))
            s_specs.append(pl.BlockSpec((None, dil, tr // dil, LANES), imap))
            o_args.append(o.reshape(batch, dil, seq // dil, GROUP_WIDTH))
            m_args.append(m.reshape(batch, dil, seq // dil, LANES))
            l_args.append(l.reshape(batch, dil, seq // dil, LANES))
    return pl.pallas_call(
        _combine_kernel,
        grid=(t // tr,),
        in_specs=o_specs + s_specs + s_specs,
        out_specs=pl.BlockSpec((tr, GROUP_WIDTH), lambda i: (i, 0)),
        out_shape=jax.ShapeDtypeStruct((t, GROUP_WIDTH), BF16),
        scratch_shapes=[pltpu.VMEM((N_GROUPS, tr, LANES), F32), pltpu.VMEM((N_GROUPS, tr, LANES), F32)],
        compiler_params=_params(),
        name="attn_combine",
    )(*o_args, *m_args, *l_args)


def _sgu_kernel(u_ref, vs_ref, g_ref, w_ref, bt_ref, o_ref):
    rows, width = vs_ref.shape
    gdim = width // SGU_GROUPS
    gv = vs_ref[...].astype(F32)
    vn = gv * lax.rsqrt(jnp.mean(gv * gv, axis=-1, keepdims=True) + NORM_EPS)
    vn = (vn * g_ref[...]).astype(BF16)
    for g in range(SGU_GROUPS):
        w = w_ref[g].astype(BF16)
        bias = bt_ref[:, g:g + 1]
        cs = slice(g * gdim, (g + 1) * gdim)
        for c in range(rows // SGU_CHUNK):
            rs = slice(c * SGU_CHUNK, (c + 1) * SGU_CHUNK)
            sp = jnp.dot(w, vn[rs, cs], preferred_element_type=F32) + bias
            o_ref[rs, cs] = (u_ref[rs, cs].astype(F32) * sp).astype(o_ref.dtype)


def _sgu(uv, sgu_norm, sgu_w, sgu_b):
    t = uv.shape[0]
    width = sgu_norm.shape[0]
    tr = 4 * SGU_CHUNK
    return pl.pallas_call(
        _sgu_kernel,
        grid=(t // tr,),
        in_specs=[
            pl.BlockSpec((tr, width), lambda i: (i, 0)),
            pl.BlockSpec((tr, width), lambda i: (i, 1)),
            pl.BlockSpec((1, width), lambda i: (0, 0)),
            pl.BlockSpec((SGU_GROUPS, SGU_CHUNK, SGU_CHUNK), lambda i: (0, 0, 0)),
            pl.BlockSpec((SGU_CHUNK, SGU_GROUPS), lambda i: (0, 0)),
        ],
        out_specs=pl.BlockSpec((tr, width), lambda i: (i, 0)),
        out_shape=jax.ShapeDtypeStruct((t, width), BF16),
        compiler_params=_params(),
        name="sgu",
    )(uv, uv, sgu_norm.reshape(1, width), sgu_w, sgu_b.T)


def _merge_kernel(at_ref, sg_ref, w_ref, ga_ref, gb_ref, *rest, n_riders):
    o_ref = rest[n_riders]
    _cast_riders(rest[:n_riders], rest[n_riders + 1:])
    ka = at_ref.shape[1]
    for r0 in range(0, at_ref.shape[0], EPILOGUE_CHUNK_ROWS):
        rows = slice(r0, r0 + EPILOGUE_CHUNK_ROWS)
        a = jnp.dot(at_ref[rows, :], w_ref[:ka, :], preferred_element_type=F32)
        b = jnp.dot(sg_ref[rows, :], w_ref[ka:, :], preferred_element_type=F32)
        o_ref[rows, :] = (ga_ref[rows, :].astype(F32) * a + gb_ref[rows, :].astype(F32) * b).astype(o_ref.dtype)


def _merge(attn, sgu, w_branch, gates, riders=()):
    t, d = attn.shape[0], w_branch.shape[1]
    tm, tn = 1024, 512
    grid = (t // tm, d // tn)
    r_in, r_out, r_shape = _rider_specs(riders, grid)
    out, *cast = pl.pallas_call(
        functools.partial(_merge_kernel, n_riders=len(riders)),
        grid=grid,
        in_specs=[
            pl.BlockSpec((tm, attn.shape[1]), lambda i, j: (i, 0)),
            pl.BlockSpec((tm, sgu.shape[1]), lambda i, j: (i, 0)),
            pl.BlockSpec((w_branch.shape[0], tn), lambda i, j: (0, j)),
            pl.BlockSpec((tm, tn), lambda i, j: (i, j)),
            pl.BlockSpec((tm, tn), lambda i, j: (i, d // tn + j)),
        ] + r_in,
        out_specs=[pl.BlockSpec((tm, tn), lambda i, j: (i, j))] + r_out,
        out_shape=[jax.ShapeDtypeStruct((t, d), BF16)] + r_shape,
        compiler_params=_params(),
        name="branch_merge",
    )(attn, sgu, w_branch, gates, gates, *riders)
    return out, cast


def _out_proj_kernel(a_ref, w_ref, x_ref, g_ref, x1_ref, xg_ref, ss_ref):
    @pl.when(pl.program_id(1) == 0)
    def _():
        ss_ref[...] = jnp.zeros_like(ss_ref)

    for r0 in range(0, a_ref.shape[0], EPILOGUE_CHUNK_ROWS):
        rows = slice(r0, r0 + EPILOGUE_CHUNK_ROWS)
        x1 = x_ref[rows, :] + jnp.dot(a_ref[rows, :], w_ref[...], preferred_element_type=F32)
        x1_ref[rows, :] = x1
        xg_ref[rows, :] = (x1 * g_ref[...]).astype(xg_ref.dtype)
        ss_ref[rows, :] += jnp.sum(x1 * x1, axis=-1, keepdims=True)


def _out_proj(a, w, x, g):
    t, k = a.shape
    d = w.shape[1]
    tm, tn = 1024, 512
    return pl.pallas_call(
        _out_proj_kernel,
        grid=(t // tm, d // tn),
        in_specs=[
            pl.BlockSpec((tm, k), lambda i, j: (i, 0)),
            pl.BlockSpec((k, tn), lambda i, j: (0, j)),
            pl.BlockSpec((tm, tn), lambda i, j: (i, j)),
            pl.BlockSpec((1, tn), lambda i, j: (0, j)),
        ],
        out_specs=[
            pl.BlockSpec((tm, tn), lambda i, j: (i, j)),
            pl.BlockSpec((tm, tn), lambda i, j: (i, j)),
            pl.BlockSpec((tm, LANES), lambda i, j: (i, 0)),
        ],
        out_shape=[
            jax.ShapeDtypeStruct((t, d), F32),
            jax.ShapeDtypeStruct((t, d), BF16),
            jax.ShapeDtypeStruct((t, LANES), F32),
        ],
        compiler_params=_params(),
        name="out_proj",
    )(a, w, x, g.reshape(1, d))


def _gate_up_kernel(xg_ref, ss_ref, wg_ref, wu_ref, o_ref, *, d_model):
    for r0 in range(0, xg_ref.shape[0], EPILOGUE_CHUNK_ROWS):
        rows = slice(r0, r0 + EPILOGUE_CHUNK_ROWS)
        inv_rms = lax.rsqrt(ss_ref[rows, 0:1] / d_model + NORM_EPS)
        xg = xg_ref[rows, :]
        g = inv_rms * jnp.dot(xg, wg_ref[...], preferred_element_type=F32)
        u = inv_rms * jnp.dot(xg, wu_ref[...], preferred_element_type=F32)
        o_ref[rows, :] = (jax.nn.silu(g) * u).astype(o_ref.dtype)


def _gate_up(xg, ss, wg, wu):
    t, k = xg.shape
    f = wg.shape[1]
    tm, tn = 2048, 256
    return pl.pallas_call(
        functools.partial(_gate_up_kernel, d_model=k),
        grid=(t // tm, f // tn),
        in_specs=[
            pl.BlockSpec((tm, k), lambda i, j: (i, 0)),
            pl.BlockSpec((tm, LANES), lambda i, j: (i, 0)),
            pl.BlockSpec((k, tn), lambda i, j: (0, j)),
            pl.BlockSpec((k, tn), lambda i, j: (0, j)),
        ],
        out_specs=pl.BlockSpec((tm, tn), lambda i, j: (i, j)),
        out_shape=jax.ShapeDtypeStruct((t, f), BF16),
        compiler_params=_params(),
        name="ffn_gate_up",
    )(xg, ss, wg, wu)


def _down_kernel(a_ref, w_ref, x_ref, o_ref):
    o_ref[...] = x_ref[...] + jnp.dot(a_ref[...], w_ref[...], preferred_element_type=F32)


def _down(a, w, x):
    t, k = a.shape
    d = w.shape[1]
    tm, tn = 512, 512
    return pl.pallas_call(
        _down_kernel,
        grid=(t // tm, d // tn),
        in_specs=[
            pl.BlockSpec((tm, k), lambda i, j: (i, 0)),
            pl.BlockSpec((k, tn), lambda i, j: (0, j)),
            pl.BlockSpec((tm, tn), lambda i, j: (i, j)),
        ],
        out_specs=pl.BlockSpec((tm, tn), lambda i, j: (i, j)),
        out_shape=jax.ShapeDtypeStruct((t, d), F32),
        compiler_params=_params(),
        name="ffn_down",
    )(a, w, x)


def _layer(x2, w, mxu_w, tables, batch, seq):
    d = x2.shape[1]
    sgu_width = w["sgu_norm"].shape[0]
    uv_first = 3 * N_GROUPS
    gates_first = uv_first + 2 * sgu_width // GROUP_WIDTH
    cast = mxu_w is None

    hs = _rmsnorm_orders(x2, w["attn_norm"], batch, seq)
    parts = []
    for g, (_, dil) in enumerate(DILATED_GROUPS):
        qk, _ = _proj(hs[dil], w["w_in"], g, N_GROUPS, 2, "rope", f"proj_qk{g}",
                      tables=tables[(seq, dil)], seq=seq)
        v, _ = _proj(hs[dil], w["w_in"], 2 * N_GROUPS + g, 1, 1, "plain", f"proj_v{g}")
        parts.append(_attn_group(qk, v, seq, g))
    attn = _combine(parts, batch, seq)

    uv, c_uv = _proj(hs[1], w["w_in"], uv_first, 1, 2 * sgu_width // GROUP_WIDTH, "gelu", "proj_uv",
                     riders=(w["w_gate"], w["w_out"]) if cast else ())
    gates, c_gates = _proj(hs[1], w["w_in"], gates_first, 1, 2 * d // GROUP_WIDTH, "sigmoid", "proj_gates",
                           riders=(w["w_up"], w["w_branch"]) if cast else ())
    if cast:
        mxu_w = {"w_gate": c_uv[0], "w_out": c_uv[1], "w_up": c_gates[0], "w_branch": c_gates[1]}
    sgu = _sgu(uv, w["sgu_norm"], w["sgu_w"], w["sgu_b"])
    merged, c_merge = _merge(attn, sgu, mxu_w["w_branch"], gates, riders=(w["w_down"],) if cast else ())
    if cast:
        mxu_w["w_down"] = c_merge[0]
    x1, xg, ss = _out_proj(merged, mxu_w["w_out"], x2, w["ffn_norm"])

    gu = _gate_up(xg, ss, mxu_w["w_gate"], mxu_w["w_up"])
    return _down(gu, mxu_w["w_down"], x1), mxu_w


def kernel(x_prompt, x_sample, attn_norm, w_in, sgu_norm, sgu_w, sgu_b, w_branch, w_out,
           ffn_norm, w_gate, w_up, w_down, final_norm):
    depth = w_in.shape[0]
    layers = [
        {
            "attn_norm": attn_norm[i], "w_in": w_in[i].astype(BF16),
            "sgu_norm": sgu_norm[i], "sgu_w": sgu_w[i], "sgu_b": sgu_b[i],
            "w_branch": w_branch[i], "w_out": w_out[i], "ffn_norm": ffn_norm[i],
            "w_gate": w_gate[i], "w_up": w_up[i], "w_down": w_down[i],
        }
        for i in range(depth)
    ]
    mxu_weights = [None] * depth
    tables = _rope_tables(sorted({x_prompt.shape[1], x_sample.shape[1]}))
    outs = []
    for x in (x_prompt, x_sample):
        batch, seq, d = x.shape
        x2 = x.reshape(batch * seq, d)
        for i, w in enumerate(layers):
            x2, mxu_weights[i] = _layer(x2, w, mxu_weights[i], tables, batch, seq)
        outs.append(_rmsnorm(x2, final_norm, F32, "final_rmsnorm").reshape(batch, seq, d))
    return tuple(outs)
```

```python
import functools
import math

import jax
import jax.numpy as jnp
from jax import lax
from jax.experimental import pallas as pl
from jax.experimental.pallas import tpu as pltpu

F32 = jnp.float32
BF16 = jnp.bfloat16

NORM_EPS = 1e-6
HEAD_DIM = 128
HEADS_PER_GROUP = 8
GROUP_WIDTH = HEADS_PER_GROUP * HEAD_DIM
DILATED_GROUPS = ((128, 1), (512, 4), (2048, 16))
N_GROUPS = len(DILATED_GROUPS)
DILATIONS = tuple(sorted({d for _, d in DILATED_GROUPS}))
ROPE_DIM = HEAD_DIM // 4
ROPE_HALF = ROPE_DIM // 2
ROPE_THETA = 500000.0
NEG_INF = -1e30
LOG2_E = math.log2(math.e)
SGU_CHUNK = 128
SGU_GROUPS = 16
LANES = 128
BF16_SUBLANES = 16
WEIGHT_COL_TILE = 512
EPILOGUE_CHUNK_ROWS = 256
V7X_VMEM_BYTES = 64 * 1024 * 1024
VMEM_LIMIT_BYTES = V7X_VMEM_BYTES - 8 * 1024 * 1024


def _params():
    return pltpu.CompilerParams(vmem_limit_bytes=VMEM_LIMIT_BYTES)


def _to_residue_major(rows, dil):
    n = rows // dil
    i = lax.broadcasted_iota(jnp.int32, (rows, rows), 0)
    j = lax.broadcasted_iota(jnp.int32, (rows, rows), 1)
    src = (i % n) * dil + i // n
    return jnp.where(j == src, 1.0, 0.0).astype(BF16)


def _to_natural(rows, dil):
    n = rows // dil
    i = lax.broadcasted_iota(jnp.int32, (rows, rows), 0)
    j = lax.broadcasted_iota(jnp.int32, (rows, rows), 1)
    src = (i % dil) * n + i // dil
    return jnp.where(j == src, 1.0, 0.0).astype(BF16)


def _rmsnorm_kernel(x_ref, g_ref, o_ref):
    x = x_ref[...]
    y = x * lax.rsqrt(jnp.mean(x * x, axis=-1, keepdims=True) + NORM_EPS)
    o_ref[...] = (y * g_ref[...]).astype(o_ref.dtype)


def _rmsnorm(x, g, out_dtype, name):
    t, d = x.shape
    tr = 256
    return pl.pallas_call(
        _rmsnorm_kernel,
        grid=(t // tr,),
        in_specs=[pl.BlockSpec((tr, d), lambda i: (i, 0)),
                  pl.BlockSpec((1, d), lambda i: (0, 0))],
        out_specs=pl.BlockSpec((tr, d), lambda i: (i, 0)),
        out_shape=jax.ShapeDtypeStruct((t, d), out_dtype),
        compiler_params=_params(),
        name=name,
    )(x, g.reshape(1, d))


def _rmsnorm_orders_kernel(x_ref, g_ref, *o_refs):
    rows = x_ref.shape[0]
    x = x_ref[...]
    y = x * lax.rsqrt(jnp.mean(x * x, axis=-1, keepdims=True) + NORM_EPS)
    y = (y * g_ref[...]).astype(BF16)
    for dil, o_ref in zip(DILATIONS, o_refs):
        if dil == 1:
            o_ref[...] = y
        else:
            n = rows // dil
            yp = jnp.dot(_to_residue_major(rows, dil), y, preferred_element_type=F32).astype(BF16)
            for r in range(dil):
                o_ref[r] = yp[r * n:(r + 1) * n, :]


def _rmsnorm_orders(x2, g, batch, seq):
    t, d = x2.shape
    tr = BF16_SUBLANES * max(DILATIONS)
    per_seq = seq // tr
    out_specs, out_shape = [], []
    for dil in DILATIONS:
        if dil == 1:
            out_specs.append(pl.BlockSpec((tr, d), lambda i: (i, 0)))
            out_shape.append(jax.ShapeDtypeStruct((t, d), BF16))
        else:
            out_specs.append(pl.BlockSpec((None, dil, tr // dil, d),
                                          lambda i: (i // per_seq, 0, i % per_seq, 0)))
            out_shape.append(jax.ShapeDtypeStruct((batch, dil, seq // dil, d), BF16))
    outs = pl.pallas_call(
        _rmsnorm_orders_kernel,
        grid=(t // tr,),
        in_specs=[pl.BlockSpec((tr, d), lambda i: (i, 0)),
                  pl.BlockSpec((1, d), lambda i: (0, 0))],
        out_specs=out_specs,
        out_shape=out_shape,
        compiler_params=_params(),
        name="attn_rmsnorm",
    )(x2, g.reshape(1, d))
    return {dil: o.reshape(t, d) for dil, o in zip(DILATIONS, outs)}


def _rope_table_kernel(f_ref, c_ref, sa_ref, sb_ref):
    rows = c_ref.shape[0]
    pos = (pl.program_id(0) * rows + lax.broadcasted_iota(jnp.int32, (rows, LANES), 0)).astype(F32)
    lane = lax.broadcasted_iota(jnp.int32, (rows, LANES), 1)
    ang = pos * f_ref[...]
    sin = jnp.sin(ang)
    c_ref[...] = jnp.cos(ang)
    sa_ref[...] = jnp.where(lane < ROPE_HALF, -sin, 0.0)
    sb_ref[...] = jnp.where((lane >= ROPE_HALF) & (lane < ROPE_DIM), sin, 0.0)


def _rope_reorder_kernel(*refs, plan):
    nat, outs = refs[:3], refs[3:]
    for k, (seq, dil) in enumerate(plan):
        n = seq // dil
        for r in range(dil):
            for src, dst in zip(nat, outs[3 * k:3 * k + 3]):
                dst[r * n:(r + 1) * n, :] = src[pl.ds(r, n, stride=dil), :]


def _rope_tables(seqs):
    inv_freq = ROPE_THETA ** (-jnp.arange(0, ROPE_DIM, 2, dtype=F32) / ROPE_DIM)
    lane_freq = jnp.concatenate([inv_freq, inv_freq, jnp.zeros((LANES - ROPE_DIM,), F32)]).reshape(1, LANES)
    s_max = max(seqs)
    rows = 256
    spec = pl.BlockSpec((rows, LANES), lambda i: (i, 0))
    shape = jax.ShapeDtypeStruct((s_max, LANES), F32)
    nat = pl.pallas_call(
        _rope_table_kernel,
        grid=(s_max // rows,),
        in_specs=[pl.BlockSpec((1, LANES), lambda i: (0, 0))],
        out_specs=[spec, spec, spec],
        out_shape=[shape, shape, shape],
        name="rope_tables",
    )(lane_freq)
    plan = [(seq, dil) for seq in seqs for dil in DILATIONS if dil != 1]
    reordered = pl.pallas_call(
        functools.partial(_rope_reorder_kernel, plan=plan),
        out_shape=[jax.ShapeDtypeStruct((seq, LANES), F32) for seq, _ in plan for _ in range(3)],
        compiler_params=_params(),
        name="rope_reorder",
    )(*nat)
    tables = {(seq, 1): tuple(nat) for seq in seqs}
    for k, key in enumerate(plan):
        tables[key] = tuple(reordered[3 * k:3 * k + 3])
    return tables


PROJ_CHUNK_ROWS = {"rope": EPILOGUE_CHUNK_ROWS, "plain": 1024, "gelu": EPILOGUE_CHUNK_ROWS,
                   "sigmoid": EPILOGUE_CHUNK_ROWS}


def _gelu(x):
    return 0.5 * x * (1.0 + lax.erf(x * math.sqrt(0.5)))


def _rider_specs(weights, grid):
    steps = grid[0] * grid[1]
    in_specs, out_specs, out_shape = [], [], []
    for w, col_tile in weights:
        rows, cols = w.shape
        n = max(c for c in range(1, steps + 1) if rows % c == 0 and (rows // c) % BF16_SUBLANES == 0)
        in_specs.append(pl.BlockSpec((rows // n, cols), lambda i, j, n=n: (jnp.minimum(i * grid[1] + j, n - 1), 0)))
        if col_tile is None:
            out_specs.append(in_specs[-1])
            out_shape.append(jax.ShapeDtypeStruct((rows, cols), BF16))
        else:
            out_specs.append(pl.BlockSpec((cols // col_tile, rows // n, col_tile),
                                          lambda i, j, n=n: (0, jnp.minimum(i * grid[1] + j, n - 1), 0)))
            out_shape.append(jax.ShapeDtypeStruct((cols // col_tile, rows, col_tile), BF16))
    return in_specs, out_specs, out_shape


def _cast_riders(in_refs, out_refs):
    for src, dst in zip(in_refs, out_refs):
        if len(dst.shape) == 2:
            dst[...] = src[...].astype(dst.dtype)
        else:
            ct = dst.shape[2]
            for c in range(dst.shape[0]):
                dst[c] = src[:, c * ct:(c + 1) * ct].astype(dst.dtype)


def _proj_kernel(*refs, kind, cm, n_riders):
    n_in = 2 + (3 if kind == "rope" else 0)
    a_ref, w_ref = refs[:2]
    o_ref = refs[n_in + n_riders]
    _cast_riders(refs[n_in:n_in + n_riders], refs[n_in + n_riders + 1:])
    for r0 in range(0, a_ref.shape[0], cm):
        rows = slice(r0, r0 + cm)
        acc = jnp.dot(a_ref[rows, :], w_ref[...], preferred_element_type=F32)
        if kind == "rope":
            c, sa, sb = (t_ref[rows, :] for t_ref in refs[2:n_in])
            for h in range(acc.shape[1] // HEAD_DIM):
                sl = slice(h * HEAD_DIM, (h + 1) * HEAD_DIM)
                z = acc[:, sl]
                rot = z * c + pltpu.roll(z, HEAD_DIM - ROPE_HALF, 1) * sa + pltpu.roll(z, ROPE_HALF, 1) * sb
                o_ref[rows, sl] = rot.astype(o_ref.dtype)
        elif kind == "gelu":
            o_ref[rows, :] = _gelu(acc).astype(o_ref.dtype)
        elif kind == "sigmoid":
            o_ref[rows, :] = jax.nn.sigmoid(acc).astype(o_ref.dtype)
        else:
            o_ref[rows, :] = acc.astype(o_ref.dtype)


def _proj(a, w, first, step, count, kind, name, tables=None, seq=None, riders=()):
    m, k = a.shape
    tm, tn = 1024, GROUP_WIDTH
    grid = (m // tm, count)
    in_specs = [pl.BlockSpec((tm, k), lambda i, j: (i, 0)),
                pl.BlockSpec((k, tn), lambda i, j: (0, first + step * j))]
    args = [a, w]
    if kind == "rope":
        per_seq = seq // tm
        tspec = pl.BlockSpec((tm, LANES), lambda i, j: (i % per_seq, 0))
        in_specs += [tspec, tspec, tspec]
        args += list(tables)
    r_in, r_out, r_shape = _rider_specs(riders, grid)
    out, *cast = pl.pallas_call(
        functools.partial(_proj_kernel, kind=kind, cm=PROJ_CHUNK_ROWS[kind], n_riders=len(riders)),
        grid=grid,
        in_specs=in_specs + r_in,
        out_specs=[pl.BlockSpec((tm, tn), lambda i, j: (i, j))] + r_out,
        out_shape=[jax.ShapeDtypeStruct((m, count * tn), BF16)] + r_shape,
        compiler_params=_params(),
        name=name,
    )(*args, *(w for w, _ in riders))
    return out, cast


def _attn_kernel(q_ref, k_ref, v_ref, o_ref, m_ref, l_ref, *, qs, win, sub_len, radius):
    tq = q_ref.shape[0]
    kv_rows = k_ref.shape[0]
    rel = (lax.broadcasted_iota(jnp.int32, (qs, win), 0) - lax.broadcasted_iota(jnp.int32, (qs, win), 1))
    scale = 1.0 / math.sqrt(HEAD_DIM)
    head_slices = [slice(h * HEAD_DIM, (h + 1) * HEAD_DIM) for h in range(HEADS_PER_GROUP)]
    m_ref[...] = jnp.zeros_like(m_ref)
    l_ref[...] = jnp.zeros_like(l_ref)
    for qb in range(tq // qs):
        rows = slice(qb * qs, (qb + 1) * qs)
        if kv_rows == sub_len:
            l0 = pl.program_id(1) * tq + qb * qs
            start = pl.multiple_of(jnp.clip(l0 - radius, 0, sub_len - win), radius)
            kv = pl.ds(start, win)
        else:
            l0 = (qb * qs) % sub_len
            start = min(max(l0 - radius, 0), sub_len - win)
            kv = pl.ds((qb * qs) // sub_len * sub_len + start, win)
        scores = [lax.dot_general(q_ref[rows, sl], k_ref[kv, sl], (((1,), (1,)), ((), ())),
                                  preferred_element_type=F32) for sl in head_slices]
        valid = jnp.abs(rel + (l0 - start)) <= radius
        for h, sl in enumerate(head_slices):
            s = jnp.where(valid, scores[h], NEG_INF)
            m = jnp.max(s, axis=-1, keepdims=True)
            p = jnp.exp2((s - m) * (scale * LOG2_E))
            l = jnp.sum(p, axis=-1, keepdims=True)
            o = jnp.dot(p.astype(BF16), v_ref[kv, sl], preferred_element_type=F32)
            o_ref[rows, sl] = o.astype(o_ref.dtype)
            m_ref[rows, h:h + 1] = m * scale
            l_ref[rows, h:h + 1] = l


def _attn_group(qk, v, seq, g):
    t = qk.shape[0]
    window, dil = DILATED_GROUPS[g]
    radius = window // (2 * dil)
    sub_len = seq // dil
    tq = 512
    qs = min(128, sub_len)
    win = min(qs + 2 * radius, sub_len)
    if sub_len >= tq:
        per_sub = sub_len // tq
        grid = (t // sub_len, per_sub)
        kv_rows = sub_len
        q_map = lambda s, j: (s * per_sub + j, 0)
        k_map = lambda s, j: (s, 1)
        v_map = lambda s, j: (s, 0)
    else:
        grid = (t // tq, 1)
        kv_rows = tq
        q_map = lambda s, j: (s, 0)
        k_map = lambda s, j: (s, 1)
        v_map = lambda s, j: (s, 0)
    return pl.pallas_call(
        functools.partial(_attn_kernel, qs=qs, win=win, sub_len=sub_len, radius=radius),
        grid=grid,
        in_specs=[
            pl.BlockSpec((tq, GROUP_WIDTH), q_map),
            pl.BlockSpec((kv_rows, GROUP_WIDTH), k_map),
            pl.BlockSpec((kv_rows, GROUP_WIDTH), v_map),
        ],
        out_specs=[
            pl.BlockSpec((tq, GROUP_WIDTH), q_map),
            pl.BlockSpec((tq, LANES), q_map),
            pl.BlockSpec((tq, LANES), q_map),
        ],
        out_shape=[
            jax.ShapeDtypeStruct((t, GROUP_WIDTH), BF16),
            jax.ShapeDtypeStruct((t, LANES), F32),
            jax.ShapeDtypeStruct((t, LANES), F32),
        ],
        compiler_params=_params(),
        name=f"attn_g{g}",
    )(qk, qk, v)


def _combine_kernel(*refs):
    o_refs, m_refs, l_refs = refs[0:N_GROUPS], refs[N_GROUPS:2 * N_GROUPS], refs[2 * N_GROUPS:3 * N_GROUPS]
    out_ref, m_scr, l_scr = refs[3 * N_GROUPS:]
    rows = out_ref.shape[0]
    os_, ms, ls = [], [], []
    for g, (_, dil) in enumerate(DILATED_GROUPS):
        if dil == 1:
            os_.append(o_refs[g][...].astype(F32))
            ms.append(m_refs[g][...])
            ls.append(l_refs[g][...])
        else:
            n = rows // dil
            for r in range(dil):
                m_scr[g, pl.ds(r, n, stride=dil), :] = m_refs[g][r]
                l_scr[g, pl.ds(r, n, stride=dil), :] = l_refs[g][r]
            ms.append(m_scr[g])
            ls.append(l_scr[g])
            og = o_refs[g][...].reshape(rows, GROUP_WIDTH)
            os_.append(jnp.dot(_to_natural(rows, dil), og, preferred_element_type=F32))
    m_max = functools.reduce(jnp.maximum, ms)
    ws = [jnp.exp(m - m_max) for m in ms]
    den = sum(w * l for w, l in zip(ws, ls))
    coef = [w / den for w in ws]
    for h in range(HEADS_PER_GROUP):
        sl = slice(h * HEAD_DIM, (h + 1) * HEAD_DIM)
        out_ref[:, sl] = sum(c[:, h:h + 1] * o[:, sl] for c, o in zip(coef, os_)).astype(out_ref.dtype)


def _combine(parts, batch, seq):
    t = batch * seq
    tr = BF16_SUBLANES * 2 * max(DILATIONS)
    per_seq = seq // tr
    o_specs, s_specs, o_args, m_args, l_args = [], [], [], [], []
    for (o, m, l), (_, dil) in zip(parts, DILATED_GROUPS):
        if dil == 1:
            o_specs.append(pl.BlockSpec((tr, GROUP_WIDTH), lambda i: (i, 0)))
            s_specs.append(pl.BlockSpec((tr, LANES), lambda i: (i, 0)))
            o_args.append(o), m_args.append(m), l_args.append(l)
        else:
            imap = lambda i: (i // per_seq, 0, i % per_seq, 0)
            o_specs.append(pl.BlockSpec((None, dil, tr // dil, GROUP_WIDTH), imap))
            s_specs.append(pl.BlockSpec((None, dil, tr // dil, LANES), imap))
            o_args.append(o.reshape(batch, dil, seq // dil, GROUP_WIDTH))
            m_args.append(m.reshape(batch, dil, seq // dil, LANES))
            l_args.append(l.reshape(batch, dil, seq // dil, LANES))
    return pl.pallas_call(
        _combine_kernel,
        grid=(t // tr,),
        in_specs=o_specs + s_specs + s_specs,
        out_specs=pl.BlockSpec((tr, GROUP_WIDTH), lambda i: (i, 0)),
        out_shape=jax.ShapeDtypeStruct((t, GROUP_WIDTH), BF16),
        scratch_shapes=[pltpu.VMEM((N_GROUPS, tr, LANES), F32), pltpu.VMEM((N_GROUPS, tr, LANES), F32)],
        compiler_params=_params(),
        name="attn_combine",
    )(*o_args, *m_args, *l_args)


def _sgu_kernel(u_ref, vs_ref, g_ref, w_ref, bt_ref, o_ref):
    rows, width = vs_ref.shape
    gdim = width // SGU_GROUPS
    gv = vs_ref[...].astype(F32)
    vn = gv * lax.rsqrt(jnp.mean(gv * gv, axis=-1, keepdims=True) + NORM_EPS)
    vn = (vn * g_ref[...]).astype(BF16)
    for g in range(SGU_GROUPS):
        w = w_ref[g].astype(BF16)
        bias = bt_ref[:, g:g + 1]
        cs = slice(g * gdim, (g + 1) * gdim)
        for c in range(rows // SGU_CHUNK):
            rs = slice(c * SGU_CHUNK, (c + 1) * SGU_CHUNK)
            sp = jnp.dot(w, vn[rs, cs], preferred_element_type=F32) + bias
            o_ref[rs, cs] = (u_ref[rs, cs].astype(F32) * sp).astype(o_ref.dtype)


def _sgu(uv, sgu_norm, sgu_w, sgu_b):
    t = uv.shape[0]
    width = sgu_norm.shape[0]
    tr = 4 * SGU_CHUNK
    return pl.pallas_call(
        _sgu_kernel,
        grid=(t // tr,),
        in_specs=[
            pl.BlockSpec((tr, width), lambda i: (i, 0)),
            pl.BlockSpec((tr, width), lambda i: (i, 1)),
            pl.BlockSpec((1, width), lambda i: (0, 0)),
            pl.BlockSpec((SGU_GROUPS, SGU_CHUNK, SGU_CHUNK), lambda i: (0, 0, 0)),
            pl.BlockSpec((SGU_CHUNK, SGU_GROUPS), lambda i: (0, 0)),
        ],
        out_specs=pl.BlockSpec((tr, width), lambda i: (i, 0)),
        out_shape=jax.ShapeDtypeStruct((t, width), BF16),
        compiler_params=_params(),
        name="sgu",
    )(uv, uv, sgu_norm.reshape(1, width), sgu_w, sgu_b.T)


def _merge_kernel(at_ref, sg_ref, w_ref, ga_ref, gb_ref, *rest, n_riders):
    o_ref = rest[n_riders]
    _cast_riders(rest[:n_riders], rest[n_riders + 1:])
    ka = at_ref.shape[1]
    for r0 in range(0, at_ref.shape[0], EPILOGUE_CHUNK_ROWS):
        rows = slice(r0, r0 + EPILOGUE_CHUNK_ROWS)
        a = jnp.dot(at_ref[rows, :], w_ref[:ka, :], preferred_element_type=F32)
        b = jnp.dot(sg_ref[rows, :], w_ref[ka:, :], preferred_element_type=F32)
        o_ref[rows, :] = (ga_ref[rows, :].astype(F32) * a + gb_ref[rows, :].astype(F32) * b).astype(o_ref.dtype)


def _merge(attn, sgu, w_branch, gates, riders=()):
    t = attn.shape[0]
    n_tiles, k, tn = w_branch.shape
    tm, d = 1024, n_tiles * tn
    grid = (t // tm, n_tiles)
    r_in, r_out, r_shape = _rider_specs(riders, grid)
    out, *cast = pl.pallas_call(
        functools.partial(_merge_kernel, n_riders=len(riders)),
        grid=grid,
        in_specs=[
            pl.BlockSpec((tm, attn.shape[1]), lambda i, j: (i, 0)),
            pl.BlockSpec((tm, sgu.shape[1]), lambda i, j: (i, 0)),
            pl.BlockSpec((None, k, tn), lambda i, j: (j, 0, 0)),
            pl.BlockSpec((tm, tn), lambda i, j: (i, j)),
            pl.BlockSpec((tm, tn), lambda i, j: (i, d // tn + j)),
        ] + r_in,
        out_specs=[pl.BlockSpec((tm, tn), lambda i, j: (i, j))] + r_out,
        out_shape=[jax.ShapeDtypeStruct((t, d), BF16)] + r_shape,
        compiler_params=_params(),
        name="branch_merge",
    )(attn, sgu, w_branch, gates, gates, *(w for w, _ in riders))
    return out, cast


def _out_proj_kernel(a_ref, w_ref, x_ref, g_ref, x1_ref, xg_ref, ss_ref):
    @pl.when(pl.program_id(1) == 0)
    def _():
        ss_ref[...] = jnp.zeros_like(ss_ref)

    for r0 in range(0, a_ref.shape[0], EPILOGUE_CHUNK_ROWS):
        rows = slice(r0, r0 + EPILOGUE_CHUNK_ROWS)
        x1 = x_ref[rows, :] + jnp.dot(a_ref[rows, :], w_ref[...], preferred_element_type=F32)
        x1_ref[rows, :] = x1
        xg_ref[rows, :] = (x1 * g_ref[...]).astype(xg_ref.dtype)
        ss_ref[rows, :] += jnp.sum(x1 * x1, axis=-1, keepdims=True)


def _out_proj(a, w, x, g):
    t, k = a.shape
    n_tiles, _, tn = w.shape
    tm, d = 1024, n_tiles * tn
    return pl.pallas_call(
        _out_proj_kernel,
        grid=(t // tm, n_tiles),
        in_specs=[
            pl.BlockSpec((tm, k), lambda i, j: (i, 0)),
            pl.BlockSpec((None, k, tn), lambda i, j: (j, 0, 0)),
            pl.BlockSpec((tm, tn), lambda i, j: (i, j)),
            pl.BlockSpec((1, tn), lambda i, j: (0, j)),
        ],
        out_specs=[
            pl.BlockSpec((tm, tn), lambda i, j: (i, j)),
            pl.BlockSpec((tm, tn), lambda i, j: (i, j)),
            pl.BlockSpec((tm, LANES), lambda i, j: (i, 0)),
        ],
        out_shape=[
            jax.ShapeDtypeStruct((t, d), F32),
            jax.ShapeDtypeStruct((t, d), BF16),
            jax.ShapeDtypeStruct((t, LANES), F32),
        ],
        compiler_params=_params(),
        name="out_proj",
    )(a, w, x, g.reshape(1, d))


def _gate_up_kernel(xg_ref, ss_ref, wg_ref, wu_ref, o_ref, *, d_model):
    for r0 in range(0, xg_ref.shape[0], EPILOGUE_CHUNK_ROWS):
        rows = slice(r0, r0 + EPILOGUE_CHUNK_ROWS)
        inv_rms = lax.rsqrt(ss_ref[rows, 0:1] / d_model + NORM_EPS)
        xg = xg_ref[rows, :]
        g = inv_rms * jnp.dot(xg, wg_ref[...], preferred_element_type=F32)
        u = inv_rms * jnp.dot(xg, wu_ref[...], preferred_element_type=F32)
        o_ref[rows, :] = (jax.nn.silu(g) * u).astype(o_ref.dtype)


def _gate_up(xg, ss, wg, wu):
    t, k = xg.shape
    f = wg.shape[1]
    tm, tn = 2048, 256
    return pl.pallas_call(
        functools.partial(_gate_up_kernel, d_model=k),
        grid=(t // tm, f // tn),
        in_specs=[
            pl.BlockSpec((tm, k), lambda i, j: (i, 0)),
            pl.BlockSpec((tm, LANES), lambda i, j: (i, 0)),
            pl.BlockSpec((k, tn), lambda i, j: (0, j)),
            pl.BlockSpec((k, tn), lambda i, j: (0, j)),
        ],
        out_specs=pl.BlockSpec((tm, tn), lambda i, j: (i, j)),
        out_shape=jax.ShapeDtypeStruct((t, f), BF16),
        compiler_params=_params(),
        name="ffn_gate_up",
    )(xg, ss, wg, wu)


def _down_kernel(a_ref, w_ref, x_ref, o_ref):
    o_ref[...] = x_ref[...] + jnp.dot(a_ref[...], w_ref[...], preferred_element_type=F32)


def _down(a, w, x):
    t, k = a.shape
    n_tiles, _, tn = w.shape
    tm, d = 512, n_tiles * tn
    return pl.pallas_call(
        _down_kernel,
        grid=(t // tm, n_tiles),
        in_specs=[
            pl.BlockSpec((tm, k), lambda i, j: (i, 0)),
            pl.BlockSpec((None, k, tn), lambda i, j: (j, 0, 0)),
            pl.BlockSpec((tm, tn), lambda i, j: (i, j)),
        ],
        out_specs=pl.BlockSpec((tm, tn), lambda i, j: (i, j)),
        out_shape=jax.ShapeDtypeStruct((t, d), F32),
        compiler_params=_params(),
        name="ffn_down",
    )(a, w, x)


def _layer(x2, w, mxu_w, tables, batch, seq):
    d = x2.shape[1]
    sgu_width = w["sgu_norm"].shape[0]
    uv_first = 3 * N_GROUPS
    gates_first = uv_first + 2 * sgu_width // GROUP_WIDTH
    cast = mxu_w is None

    hs = _rmsnorm_orders(x2, w["attn_norm"], batch, seq)
    parts = []
    for g, (_, dil) in enumerate(DILATED_GROUPS):
        qk, _ = _proj(hs[dil], w["w_in"], g, N_GROUPS, 2, "rope", f"proj_qk{g}",
                      tables=tables[(seq, dil)], seq=seq)
        v, _ = _proj(hs[dil], w["w_in"], 2 * N_GROUPS + g, 1, 1, "plain", f"proj_v{g}")
        parts.append(_attn_group(qk, v, seq, g))
    attn = _combine(parts, batch, seq)

    uv, c_uv = _proj(hs[1], w["w_in"], uv_first, 1, 2 * sgu_width // GROUP_WIDTH, "gelu", "proj_uv",
                     riders=((w["w_gate"], None), (w["w_out"], WEIGHT_COL_TILE)) if cast else ())
    gates, c_gates = _proj(hs[1], w["w_in"], gates_first, 1, 2 * d // GROUP_WIDTH, "sigmoid", "proj_gates",
                           riders=((w["w_up"], None), (w["w_branch"], WEIGHT_COL_TILE)) if cast else ())
    if cast:
        mxu_w = {"w_gate": c_uv[0], "w_out": c_uv[1], "w_up": c_gates[0], "w_branch": c_gates[1]}
    sgu = _sgu(uv, w["sgu_norm"], w["sgu_w"], w["sgu_b"])
    merged, c_merge = _merge(attn, sgu, mxu_w["w_branch"], gates, riders=((w["w_down"], WEIGHT_COL_TILE),) if cast else ())
    if cast:
        mxu_w["w_down"] = c_merge[0]
    x1, xg, ss = _out_proj(merged, mxu_w["w_out"], x2, w["ffn_norm"])

    gu = _gate_up(xg, ss, mxu_w["w_gate"], mxu_w["w_up"])
    return _down(gu, mxu_w["w_down"], x1), mxu_w


def kernel(x_prompt, x_sample, attn_norm, w_in, sgu_norm, sgu_w, sgu_b, w_branch, w_out,
           ffn_norm, w_gate, w_up, w_down, final_norm):
    depth = w_in.shape[0]
    layers = [
        {
            "attn_norm": attn_norm[i], "w_in": w_in[i].astype(BF16),
            "sgu_norm": sgu_norm[i], "sgu_w": sgu_w[i], "sgu_b": sgu_b[i],
            "w_branch": w_branch[i], "w_out": w_out[i], "ffn_norm": ffn_norm[i],
            "w_gate": w_gate[i], "w_up": w_up[i], "w_down": w_down[i],
        }
        for i in range(depth)
    ]
    mxu_weights = [None] * depth
    tables = _rope_tables(sorted({x_prompt.shape[1], x_sample.shape[1]}))
    outs = []
    for x in (x_prompt, x_sample):
        batch, seq, d = x.shape
        x2 = x.reshape(batch * seq, d)
        for i, w in enumerate(layers):
            x2, mxu_weights[i] = _layer(x2, w, mxu_weights[i], tables, batch, seq)
        outs.append(_rmsnorm(x2, final_norm, F32, "final_rmsnorm").reshape(batch, seq, d))
    return tuple(outs)
```

```python
import functools
import math

import jax
import jax.numpy as jnp
from jax import lax
from jax.experimental import pallas as pl
from jax.experimental.pallas import tpu as pltpu

F32 = jnp.float32
BF16 = jnp.bfloat16

NORM_EPS = 1e-6
HEAD_DIM = 128
HEADS_PER_GROUP = 8
GROUP_WIDTH = HEADS_PER_GROUP * HEAD_DIM
DILATED_GROUPS = ((128, 1), (512, 4), (2048, 16))
N_GROUPS = len(DILATED_GROUPS)
DILATIONS = tuple(sorted({d for _, d in DILATED_GROUPS}))
ROPE_DIM = HEAD_DIM // 4
ROPE_HALF = ROPE_DIM // 2
ROPE_THETA = 500000.0
NEG_INF = -1e30
LOG2_E = math.log2(math.e)
SGU_CHUNK = 128
SGU_GROUPS = 16
LANES = 128
BF16_SUBLANES = 16
WEIGHT_COL_TILE = 512
OUT_PROJ_COL_TILE = 1024
EPILOGUE_CHUNK_ROWS = 128
V7X_VMEM_BYTES = 64 * 1024 * 1024
VMEM_LIMIT_BYTES = V7X_VMEM_BYTES - 8 * 1024 * 1024


def _params():
    return pltpu.CompilerParams(vmem_limit_bytes=VMEM_LIMIT_BYTES)


def _to_residue_major(rows, dil):
    n = rows // dil
    i = lax.broadcasted_iota(jnp.int32, (rows, rows), 0)
    j = lax.broadcasted_iota(jnp.int32, (rows, rows), 1)
    src = (i % n) * dil + i // n
    return jnp.where(j == src, 1.0, 0.0).astype(BF16)


def _to_natural(rows, dil):
    n = rows // dil
    i = lax.broadcasted_iota(jnp.int32, (rows, rows), 0)
    j = lax.broadcasted_iota(jnp.int32, (rows, rows), 1)
    src = (i % dil) * n + i // dil
    return jnp.where(j == src, 1.0, 0.0).astype(BF16)


def _rmsnorm_kernel(x_ref, g_ref, o_ref):
    x = x_ref[...]
    y = x * lax.rsqrt(jnp.mean(x * x, axis=-1, keepdims=True) + NORM_EPS)
    o_ref[...] = (y * g_ref[...]).astype(o_ref.dtype)


def _rmsnorm(x, g, out_dtype, name):
    t, d = x.shape
    tr = 256
    return pl.pallas_call(
        _rmsnorm_kernel,
        grid=(t // tr,),
        in_specs=[pl.BlockSpec((tr, d), lambda i: (i, 0)),
                  pl.BlockSpec((1, d), lambda i: (0, 0))],
        out_specs=pl.BlockSpec((tr, d), lambda i: (i, 0)),
        out_shape=jax.ShapeDtypeStruct((t, d), out_dtype),
        compiler_params=_params(),
        name=name,
    )(x, g.reshape(1, d))


def _rmsnorm_orders_kernel(x_ref, g_ref, *o_refs):
    rows = x_ref.shape[0]
    x = x_ref[...]
    y = x * lax.rsqrt(jnp.mean(x * x, axis=-1, keepdims=True) + NORM_EPS)
    y = (y * g_ref[...]).astype(BF16)
    for dil, o_ref in zip(DILATIONS, o_refs):
        if dil == 1:
            o_ref[...] = y
        else:
            n = rows // dil
            yp = jnp.dot(_to_residue_major(rows, dil), y, preferred_element_type=F32).astype(BF16)
            for r in range(dil):
                o_ref[r] = yp[r * n:(r + 1) * n, :]


def _rmsnorm_orders(x2, g, batch, seq):
    t, d = x2.shape
    tr = BF16_SUBLANES * max(DILATIONS)
    per_seq = seq // tr
    out_specs, out_shape = [], []
    for dil in DILATIONS:
        if dil == 1:
            out_specs.append(pl.BlockSpec((tr, d), lambda i: (i, 0)))
            out_shape.append(jax.ShapeDtypeStruct((t, d), BF16))
        else:
            out_specs.append(pl.BlockSpec((None, dil, tr // dil, d),
                                          lambda i: (i // per_seq, 0, i % per_seq, 0)))
            out_shape.append(jax.ShapeDtypeStruct((batch, dil, seq // dil, d), BF16))
    outs = pl.pallas_call(
        _rmsnorm_orders_kernel,
        grid=(t // tr,),
        in_specs=[pl.BlockSpec((tr, d), lambda i: (i, 0)),
                  pl.BlockSpec((1, d), lambda i: (0, 0))],
        out_specs=out_specs,
        out_shape=out_shape,
        compiler_params=_params(),
        name="attn_rmsnorm",
    )(x2, g.reshape(1, d))
    return {dil: o.reshape(t, d) for dil, o in zip(DILATIONS, outs)}


def _rope_table_kernel(f_ref, c_ref, sa_ref, sb_ref):
    rows = c_ref.shape[0]
    pos = (pl.program_id(0) * rows + lax.broadcasted_iota(jnp.int32, (rows, LANES), 0)).astype(F32)
    lane = lax.broadcasted_iota(jnp.int32, (rows, LANES), 1)
    ang = pos * f_ref[...]
    sin = jnp.sin(ang)
    c_ref[...] = jnp.cos(ang)
    sa_ref[...] = jnp.where(lane < ROPE_HALF, -sin, 0.0)
    sb_ref[...] = jnp.where((lane >= ROPE_HALF) & (lane < ROPE_DIM), sin, 0.0)


def _rope_reorder_kernel(*refs, plan):
    nat, outs = refs[:3], refs[3:]
    for k, (seq, dil) in enumerate(plan):
        n = seq // dil
        for r in range(dil):
            for src, dst in zip(nat, outs[3 * k:3 * k + 3]):
                dst[r * n:(r + 1) * n, :] = src[pl.ds(r, n, stride=dil), :]


def _rope_tables(seqs):
    inv_freq = ROPE_THETA ** (-jnp.arange(0, ROPE_DIM, 2, dtype=F32) / ROPE_DIM)
    lane_freq = jnp.concatenate([inv_freq, inv_freq, jnp.zeros((LANES - ROPE_DIM,), F32)]).reshape(1, LANES)
    s_max = max(seqs)
    rows = 256
    spec = pl.BlockSpec((rows, LANES), lambda i: (i, 0))
    shape = jax.ShapeDtypeStruct((s_max, LANES), F32)
    nat = pl.pallas_call(
        _rope_table_kernel,
        grid=(s_max // rows,),
        in_specs=[pl.BlockSpec((1, LANES), lambda i: (0, 0))],
        out_specs=[spec, spec, spec],
        out_shape=[shape, shape, shape],
        name="rope_tables",
    )(lane_freq)
    plan = [(seq, dil) for seq in seqs for dil in DILATIONS if dil != 1]
    reordered = pl.pallas_call(
        functools.partial(_rope_reorder_kernel, plan=plan),
        out_shape=[jax.ShapeDtypeStruct((seq, LANES), F32) for seq, _ in plan for _ in range(3)],
        compiler_params=_params(),
        name="rope_reorder",
    )(*nat)
    tables = {(seq, 1): tuple(nat) for seq in seqs}
    for k, key in enumerate(plan):
        tables[key] = tuple(reordered[3 * k:3 * k + 3])
    return tables


PROJ_CHUNK_ROWS = {"rope": EPILOGUE_CHUNK_ROWS, "plain": 1024, "gelu": EPILOGUE_CHUNK_ROWS,
                   "sigmoid": EPILOGUE_CHUNK_ROWS}


def _gelu(x):
    return 0.5 * x * (1.0 + lax.erf(x * math.sqrt(0.5)))


def _rider_specs(weights, grid):
    steps = grid[0] * grid[1]
    in_specs, out_specs, out_shape = [], [], []
    for w, col_tile in weights:
        rows, cols = w.shape
        n = max(c for c in range(1, steps + 1) if rows % c == 0 and (rows // c) % BF16_SUBLANES == 0)
        in_specs.append(pl.BlockSpec((rows // n, cols), lambda i, j, n=n: (jnp.minimum(i * grid[1] + j, n - 1), 0)))
        if col_tile is None:
            out_specs.append(in_specs[-1])
            out_shape.append(jax.ShapeDtypeStruct((rows, cols), BF16))
        else:
            out_specs.append(pl.BlockSpec((cols // col_tile, rows // n, col_tile),
                                          lambda i, j, n=n: (0, jnp.minimum(i * grid[1] + j, n - 1), 0)))
            out_shape.append(jax.ShapeDtypeStruct((cols // col_tile, rows, col_tile), BF16))
    return in_specs, out_specs, out_shape


def _cast_riders(in_refs, out_refs):
    for src, dst in zip(in_refs, out_refs):
        if len(dst.shape) == 2:
            dst[...] = src[...].astype(dst.dtype)
        else:
            ct = dst.shape[2]
            for c in range(dst.shape[0]):
                dst[c] = src[:, c * ct:(c + 1) * ct].astype(dst.dtype)


def _proj_kernel(*refs, kind, cm, n_riders, norm_rider):
    n_in = 2 + (3 if kind == "rope" else 0)
    n_all_in = n_in + n_riders + (2 if norm_rider else 0)
    a_ref, w_ref = refs[:2]
    o_ref = refs[n_all_in]
    _cast_riders(refs[n_in:n_in + n_riders], refs[n_all_in + 1:n_all_in + 1 + n_riders])
    if norm_rider:
        _rmsnorm_kernel(refs[n_all_in - 2], refs[n_all_in - 1], refs[-1])
    for r0 in range(0, a_ref.shape[0], cm):
        rows = slice(r0, r0 + cm)
        acc = jnp.dot(a_ref[rows, :], w_ref[...], preferred_element_type=F32)
        if kind == "rope":
            c, sa, sb = (t_ref[rows, :] for t_ref in refs[2:n_in])
            for h in range(acc.shape[1] // HEAD_DIM):
                sl = slice(h * HEAD_DIM, (h + 1) * HEAD_DIM)
                z = acc[:, sl]
                rot = z * c + pltpu.roll(z, HEAD_DIM - ROPE_HALF, 1) * sa + pltpu.roll(z, ROPE_HALF, 1) * sb
                o_ref[rows, sl] = rot.astype(o_ref.dtype)
        elif kind == "gelu":
            o_ref[rows, :] = _gelu(acc).astype(o_ref.dtype)
        elif kind == "sigmoid":
            o_ref[rows, :] = jax.nn.sigmoid(acc).astype(o_ref.dtype)
        else:
            o_ref[rows, :] = acc.astype(o_ref.dtype)


def _proj(a, w, first, step, count, kind, name, tables=None, seq=None, riders=(), norm_rider=None):
    m, k = a.shape
    tm, tn = 1024, GROUP_WIDTH
    grid = (m // tm, count)
    in_specs = [pl.BlockSpec((tm, k), lambda i, j: (i, 0)),
                pl.BlockSpec((k, tn), lambda i, j: (0, first + step * j))]
    args = [a, w]
    if kind == "rope":
        per_seq = seq // tm
        tspec = pl.BlockSpec((tm, LANES), lambda i, j: (i % per_seq, 0))
        in_specs += [tspec, tspec, tspec]
        args += list(tables)
    r_in, r_out, r_shape = _rider_specs(riders, grid)
    n_in, n_out, extra = [], [], []
    if norm_rider is not None:
        y, g = norm_rider
        rows = y.shape[0] // (grid[0] * grid[1])
        yspec = pl.BlockSpec((rows, y.shape[1]), lambda i, j: (i * count + j, 0))
        n_in = [yspec, pl.BlockSpec((1, y.shape[1]), lambda i, j: (0, 0))]
        n_out = [yspec]
        extra = [y, g.reshape(1, -1)]
    out, *rest = pl.pallas_call(
        functools.partial(_proj_kernel, kind=kind, cm=PROJ_CHUNK_ROWS[kind], n_riders=len(riders),
                          norm_rider=norm_rider is not None),
        grid=grid,
        in_specs=in_specs + r_in + n_in,
        out_specs=[pl.BlockSpec((tm, tn), lambda i, j: (i, j))] + r_out + n_out,
        out_shape=[jax.ShapeDtypeStruct((m, count * tn), BF16)] + r_shape
        + ([jax.ShapeDtypeStruct(norm_rider[0].shape, F32)] if norm_rider is not None else []),
        compiler_params=_params(),
        name=name,
    )(*args, *(w for w, _ in riders), *extra)
    normed = rest.pop() if norm_rider is not None else None
    return out, rest, normed


def _attn_kernel(q_ref, k_ref, v_ref, o_ref, m_ref, l_ref, *, qs, win, sub_len, radius):
    tq = q_ref.shape[0]
    kv_rows = k_ref.shape[0]
    rel = (lax.broadcasted_iota(jnp.int32, (qs, win), 0) - lax.broadcasted_iota(jnp.int32, (qs, win), 1))
    scale = 1.0 / math.sqrt(HEAD_DIM)
    head_slices = [slice(h * HEAD_DIM, (h + 1) * HEAD_DIM) for h in range(HEADS_PER_GROUP)]
    m_ref[...] = jnp.zeros_like(m_ref)
    l_ref[...] = jnp.zeros_like(l_ref)
    for qb in range(tq // qs):
        rows = slice(qb * qs, (qb + 1) * qs)
        if kv_rows == sub_len:
            l0 = pl.program_id(1) * tq + qb * qs
            start = pl.multiple_of(jnp.clip(l0 - radius, 0, sub_len - win), radius)
            kv = pl.ds(start, win)
        else:
            l0 = (qb * qs) % sub_len
            start = min(max(l0 - radius, 0), sub_len - win)
            kv = pl.ds((qb * qs) // sub_len * sub_len + start, win)
        scores = [lax.dot_general(q_ref[rows, sl], k_ref[kv, sl], (((1,), (1,)), ((), ())),
                                  preferred_element_type=F32) for sl in head_slices]
        valid = jnp.abs(rel + (l0 - start)) <= radius
        for h, sl in enumerate(head_slices):
            s = jnp.where(valid, scores[h], NEG_INF)
            m = jnp.max(s, axis=-1, keepdims=True)
            p = jnp.exp2((s - m) * (scale * LOG2_E))
            l = jnp.sum(p, axis=-1, keepdims=True)
            o = jnp.dot(p.astype(BF16), v_ref[kv, sl], preferred_element_type=F32)
            o_ref[rows, sl] = o.astype(o_ref.dtype)
            m_ref[rows, h:h + 1] = m * scale
            l_ref[rows, h:h + 1] = l


def _attn_group(qk, v, seq, g):
    t = qk.shape[0]
    window, dil = DILATED_GROUPS[g]
    radius = window // (2 * dil)
    sub_len = seq // dil
    tq = 512
    qs = min(128, sub_len)
    win = min(qs + 2 * radius, sub_len)
    if sub_len >= tq:
        per_sub = sub_len // tq
        grid = (t // sub_len, per_sub)
        kv_rows = sub_len
        q_map = lambda s, j: (s * per_sub + j, 0)
        k_map = lambda s, j: (s, 1)
        v_map = lambda s, j: (s, 0)
    else:
        grid = (t // tq, 1)
        kv_rows = tq
        q_map = lambda s, j: (s, 0)
        k_map = lambda s, j: (s, 1)
        v_map = lambda s, j: (s, 0)
    return pl.pallas_call(
        functools.partial(_attn_kernel, qs=qs, win=win, sub_len=sub_len, radius=radius),
        grid=grid,
        in_specs=[
            pl.BlockSpec((tq, GROUP_WIDTH), q_map),
            pl.BlockSpec((kv_rows, GROUP_WIDTH), k_map),
            pl.BlockSpec((kv_rows, GROUP_WIDTH), v_map),
        ],
        out_specs=[
            pl.BlockSpec((tq, GROUP_WIDTH), q_map),
            pl.BlockSpec((tq, LANES), q_map),
            pl.BlockSpec((tq, LANES), q_map),
        ],
        out_shape=[
            jax.ShapeDtypeStruct((t, GROUP_WIDTH), BF16),
            jax.ShapeDtypeStruct((t, LANES), F32),
            jax.ShapeDtypeStruct((t, LANES), F32),
        ],
        compiler_params=_params(),
        name=f"attn_g{g}",
    )(qk, qk, v)


def _combine_kernel(*refs):
    o_refs, m_refs, l_refs = refs[0:N_GROUPS], refs[N_GROUPS:2 * N_GROUPS], refs[2 * N_GROUPS:3 * N_GROUPS]
    out_ref, m_scr, l_scr = refs[3 * N_GROUPS:]
    rows = out_ref.shape[0]
    os_, ms, ls = [], [], []
    for g, (_, dil) in enumerate(DILATED_GROUPS):
        if dil == 1:
            os_.append(o_refs[g][...].astype(F32))
            ms.append(m_refs[g][...])
            ls.append(l_refs[g][...])
        else:
            n = rows // dil
            for r in range(dil):
                m_scr[g, pl.ds(r, n, stride=dil), :] = m_refs[g][r]
                l_scr[g, pl.ds(r, n, stride=dil), :] = l_refs[g][r]
            ms.append(m_scr[g])
            ls.append(l_scr[g])
            og = o_refs[g][...].reshape(rows, GROUP_WIDTH)
            os_.append(jnp.dot(_to_natural(rows, dil), og, preferred_element_type=F32))
    m_max = functools.reduce(jnp.maximum, ms)
    ws = [jnp.exp(m - m_max) for m in ms]
    den = sum(w * l for w, l in zip(ws, ls))
    coef = [w / den for w in ws]
    for h in range(HEADS_PER_GROUP):
        sl = slice(h * HEAD_DIM, (h + 1) * HEAD_DIM)
        out_ref[:, sl] = sum(c[:, h:h + 1] * o[:, sl] for c, o in zip(coef, os_)).astype(out_ref.dtype)


def _combine(parts, batch, seq):
    t = batch * seq
    tr = BF16_SUBLANES * 2 * max(DILATIONS)
    per_seq = seq // tr
    o_specs, s_specs, o_args, m_args, l_args = [], [], [], [], []
    for (o, m, l), (_, dil) in zip(parts, DILATED_GROUPS):
        if dil == 1:
            o_specs.append(pl.BlockSpec((tr, GROUP_WIDTH), lambda i: (i, 0)))
            s_specs.append(pl.BlockSpec((tr, LANES), lambda i: (i, 0)))
            o_args.append(o), m_args.append(m), l_args.append(l)
        else:
            imap = lambda i: (i // per_seq, 0, i % per_seq, 0)
            o_specs.append(pl.BlockSpec((None, dil, tr // dil, GROUP_WIDTH), imap))
            s_specs.append(pl.BlockSpec((None, dil, tr // dil, LANES), imap))
            o_args.append(o.reshape(batch, dil, seq // dil, GROUP_WIDTH))
            m_args.append(m.reshape(batch, dil, seq // dil, LANES))
            l_args.append(l.reshape(batch, dil, seq // dil, LANES))
    return pl.pallas_call(
        _combine_kernel,
        grid=(t // tr,),
        in_specs=o_specs + s_specs + s_specs,
        out_specs=pl.BlockSpec((tr, GROUP_WIDTH), lambda i: (i, 0)),
        out_shape=jax.ShapeDtypeStruct((t, GROUP_WIDTH), BF16),
        scratch_shapes=[pltpu.VMEM((N_GROUPS, tr, LANES), F32), pltpu.VMEM((N_GROUPS, tr, LANES), F32)],
        compiler_params=_params(),
        name="attn_combine",
    )(*o_args, *m_args, *l_args)


def _sgu_kernel(u_ref, vs_ref, g_ref, w_ref, bt_ref, o_ref):
    rows, width = vs_ref.shape
    gdim = width // SGU_GROUPS
    gv = vs_ref[...].astype(F32)
    vn = gv * lax.rsqrt(jnp.mean(gv * gv, axis=-1, keepdims=True) + NORM_EPS)
    vn = (vn * g_ref[...]).astype(BF16)
    for g in range(SGU_GROUPS):
        w = w_ref[g].astype(BF16)
        bias = bt_ref[:, g:g + 1]
        cs = slice(g * gdim, (g + 1) * gdim)
        for c in range(rows // SGU_CHUNK):
            rs = slice(c * SGU_CHUNK, (c + 1) * SGU_CHUNK)
            sp = jnp.dot(w, vn[rs, cs], preferred_element_type=F32) + bias
            o_ref[rs, cs] = (u_ref[rs, cs].astype(F32) * sp).astype(o_ref.dtype)


def _sgu(uv, sgu_norm, sgu_w, sgu_b):
    t = uv.shape[0]
    width = sgu_norm.shape[0]
    tr = 4 * SGU_CHUNK
    return pl.pallas_call(
        _sgu_kernel,
        grid=(t // tr,),
        in_specs=[
            pl.BlockSpec((tr, width), lambda i: (i, 0)),
            pl.BlockSpec((tr, width), lambda i: (i, 1)),
            pl.BlockSpec((1, width), lambda i: (0, 0)),
            pl.BlockSpec((SGU_GROUPS, SGU_CHUNK, SGU_CHUNK), lambda i: (0, 0, 0)),
            pl.BlockSpec((SGU_CHUNK, SGU_GROUPS), lambda i: (0, 0)),
        ],
        out_specs=pl.BlockSpec((tr, width), lambda i: (i, 0)),
        out_shape=jax.ShapeDtypeStruct((t, width), BF16),
        compiler_params=_params(),
        name="sgu",
    )(uv, uv, sgu_norm.reshape(1, width), sgu_w, sgu_b.T)


def _merge_kernel(at_ref, sg_ref, w_ref, ga_ref, gb_ref, *rest, n_riders):
    o_ref = rest[n_riders]
    _cast_riders(rest[:n_riders], rest[n_riders + 1:])
    ka = at_ref.shape[1]
    for r0 in range(0, at_ref.shape[0], EPILOGUE_CHUNK_ROWS):
        rows = slice(r0, r0 + EPILOGUE_CHUNK_ROWS)
        a = jnp.dot(at_ref[rows, :], w_ref[:ka, :], preferred_element_type=F32)
        b = jnp.dot(sg_ref[rows, :], w_ref[ka:, :], preferred_element_type=F32)
        o_ref[rows, :] = (ga_ref[rows, :].astype(F32) * a + gb_ref[rows, :].astype(F32) * b).astype(o_ref.dtype)


def _merge(attn, sgu, w_branch, gates, riders=()):
    t = attn.shape[0]
    n_tiles, k, tn = w_branch.shape
    tm, d = 1024, n_tiles * tn
    grid = (t // tm, n_tiles)
    r_in, r_out, r_shape = _rider_specs(riders, grid)
    out, *cast = pl.pallas_call(
        functools.partial(_merge_kernel, n_riders=len(riders)),
        grid=grid,
        in_specs=[
            pl.BlockSpec((tm, attn.shape[1]), lambda i, j: (i, 0)),
            pl.BlockSpec((tm, sgu.shape[1]), lambda i, j: (i, 0)),
            pl.BlockSpec((None, k, tn), lambda i, j: (j, 0, 0)),
            pl.BlockSpec((tm, tn), lambda i, j: (i, j)),
            pl.BlockSpec((tm, tn), lambda i, j: (i, d // tn + j)),
        ] + r_in,
        out_specs=[pl.BlockSpec((tm, tn), lambda i, j: (i, j))] + r_out,
        out_shape=[jax.ShapeDtypeStruct((t, d), BF16)] + r_shape,
        compiler_params=_params(),
        name="branch_merge",
    )(attn, sgu, w_branch, gates, gates, *(w for w, _ in riders))
    return out, cast


def _out_proj_kernel(a_ref, w_ref, x_ref, g_ref, x1_ref, xg_ref, ss_ref):
    @pl.when(pl.program_id(1) == 0)
    def _():
        ss_ref[...] = jnp.zeros_like(ss_ref)

    for r0 in range(0, a_ref.shape[0], EPILOGUE_CHUNK_ROWS):
        rows = slice(r0, r0 + EPILOGUE_CHUNK_ROWS)
        x1 = x_ref[rows, :] + jnp.dot(a_ref[rows, :], w_ref[...], preferred_element_type=F32)
        x1_ref[rows, :] = x1
        xg_ref[rows, :] = (x1 * g_ref[...]).astype(xg_ref.dtype)
        ss_ref[rows, :] += jnp.sum(x1 * x1, axis=-1, keepdims=True)


def _out_proj(a, w, x, g):
    t, k = a.shape
    n_tiles, _, tn = w.shape
    tm, d = 1024, n_tiles * tn
    return pl.pallas_call(
        _out_proj_kernel,
        grid=(t // tm, n_tiles),
        in_specs=[
            pl.BlockSpec((tm, k), lambda i, j: (i, 0)),
            pl.BlockSpec((None, k, tn), lambda i, j: (j, 0, 0)),
            pl.BlockSpec((tm, tn), lambda i, j: (i, j)),
            pl.BlockSpec((1, tn), lambda i, j: (0, j)),
        ],
        out_specs=[
            pl.BlockSpec((tm, tn), lambda i, j: (i, j)),
            pl.BlockSpec((tm, tn), lambda i, j: (i, j)),
            pl.BlockSpec((tm, LANES), lambda i, j: (i, 0)),
        ],
        out_shape=[
            jax.ShapeDtypeStruct((t, d), F32),
            jax.ShapeDtypeStruct((t, d), BF16),
            jax.ShapeDtypeStruct((t, LANES), F32),
        ],
        compiler_params=_params(),
        name="out_proj",
    )(a, w, x, g.reshape(1, d))


def _gate_up_kernel(xg_ref, ss_ref, wg_ref, wu_ref, o_ref, *, d_model):
    for r0 in range(0, xg_ref.shape[0], EPILOGUE_CHUNK_ROWS):
        rows = slice(r0, r0 + EPILOGUE_CHUNK_ROWS)
        inv_rms = lax.rsqrt(ss_ref[rows, 0:1] / d_model + NORM_EPS)
        xg = xg_ref[rows, :]
        g = inv_rms * jnp.dot(xg, wg_ref[...], preferred_element_type=F32)
        u = inv_rms * jnp.dot(xg, wu_ref[...], preferred_element_type=F32)
        o_ref[rows, :] = (jax.nn.silu(g) * u).astype(o_ref.dtype)


def _gate_up(xg, ss, wg, wu):
    t, k = xg.shape
    f = wg.shape[1]
    tm, tn = 2048, 256
    return pl.pallas_call(
        functools.partial(_gate_up_kernel, d_model=k),
        grid=(t // tm, f // tn),
        in_specs=[
            pl.BlockSpec((tm, k), lambda i, j: (i, 0)),
            pl.BlockSpec((tm, LANES), lambda i, j: (i, 0)),
            pl.BlockSpec((k, tn), lambda i, j: (0, j)),
            pl.BlockSpec((k, tn), lambda i, j: (0, j)),
        ],
        out_specs=pl.BlockSpec((tm, tn), lambda i, j: (i, j)),
        out_shape=jax.ShapeDtypeStruct((t, f), BF16),
        compiler_params=_params(),
        name="ffn_gate_up",
    )(xg, ss, wg, wu)


def _down_kernel(a_ref, w_ref, x_ref, o_ref):
    o_ref[...] = x_ref[...] + jnp.dot(a_ref[...], w_ref[...], preferred_element_type=F32)


def _down(a, w, x):
    t, k = a.shape
    n_tiles, _, tn = w.shape
    tm, d = 512, n_tiles * tn
    return pl.pallas_call(
        _down_kernel,
        grid=(t // tm, n_tiles),
        in_specs=[
            pl.BlockSpec((tm, k), lambda i, j: (i, 0)),
            pl.BlockSpec((None, k, tn), lambda i, j: (j, 0, 0)),
            pl.BlockSpec((tm, tn), lambda i, j: (i, j)),
        ],
        out_specs=pl.BlockSpec((tm, tn), lambda i, j: (i, j)),
        out_shape=jax.ShapeDtypeStruct((t, d), F32),
        compiler_params=_params(),
        name="ffn_down",
    )(a, w, x)


def _layer(x2, w, mxu_w, tables, batch, seq, norm_rider=None):
    d = x2.shape[1]
    sgu_width = w["sgu_norm"].shape[0]
    uv_first = 3 * N_GROUPS
    gates_first = uv_first + 2 * sgu_width // GROUP_WIDTH
    cast = mxu_w is None

    hs = _rmsnorm_orders(x2, w["attn_norm"], batch, seq)
    parts = []
    for g, (_, dil) in enumerate(DILATED_GROUPS):
        qk, _, _ = _proj(hs[dil], w["w_in"], g, N_GROUPS, 2, "rope", f"proj_qk{g}",
                         tables=tables[(seq, dil)], seq=seq)
        v, _, _ = _proj(hs[dil], w["w_in"], 2 * N_GROUPS + g, 1, 1, "plain", f"proj_v{g}")
        parts.append(_attn_group(qk, v, seq, g))
    attn = _combine(parts, batch, seq)

    uv, c_uv, normed = _proj(
        hs[1], w["w_in"], uv_first, 1, 2 * sgu_width // GROUP_WIDTH, "gelu", "proj_uv",
        riders=((w["w_gate"], None), (w["w_out"], OUT_PROJ_COL_TILE)) if cast else (), norm_rider=norm_rider)
    gates, c_gates, _ = _proj(
        hs[1], w["w_in"], gates_first, 1, 2 * d // GROUP_WIDTH, "sigmoid", "proj_gates",
        riders=((w["w_up"], None), (w["w_branch"], WEIGHT_COL_TILE)) if cast else ())
    if cast:
        mxu_w = {"w_gate": c_uv[0], "w_out": c_uv[1], "w_up": c_gates[0], "w_branch": c_gates[1]}
    sgu = _sgu(uv, w["sgu_norm"], w["sgu_w"], w["sgu_b"])
    merged, c_merge = _merge(attn, sgu, mxu_w["w_branch"], gates,
                             riders=((w["w_down"], WEIGHT_COL_TILE),) if cast else ())
    if cast:
        mxu_w["w_down"] = c_merge[0]
    x1, xg, ss = _out_proj(merged, mxu_w["w_out"], x2, w["ffn_norm"])

    gu = _gate_up(xg, ss, mxu_w["w_gate"], mxu_w["w_up"])
    return _down(gu, mxu_w["w_down"], x1), mxu_w, normed


def kernel(x_prompt, x_sample, attn_norm, w_in, sgu_norm, sgu_w, sgu_b, w_branch, w_out,
           ffn_norm, w_gate, w_up, w_down, final_norm):
    depth = w_in.shape[0]
    layers = [
        {
            "attn_norm": attn_norm[i], "w_in": w_in[i].astype(BF16),
            "sgu_norm": sgu_norm[i], "sgu_w": sgu_w[i], "sgu_b": sgu_b[i],
            "w_branch": w_branch[i], "w_out": w_out[i], "ffn_norm": ffn_norm[i],
            "w_gate": w_gate[i], "w_up": w_up[i], "w_down": w_down[i],
        }
        for i in range(depth)
    ]
    mxu_weights = [None] * depth
    tables = _rope_tables(sorted({x_prompt.shape[1], x_sample.shape[1]}))
    outs = []
    pending = None
    for x in (x_prompt, x_sample):
        batch, seq, d = x.shape
        x2 = x.reshape(batch * seq, d)
        for i, w in enumerate(layers):
            rider = (pending[0], final_norm) if pending is not None and i == 0 else None
            x2, mxu_weights[i], normed = _layer(x2, w, mxu_weights[i], tables, batch, seq, norm_rider=rider)
            if rider is not None:
                outs.append(normed.reshape(pending[1]))
                pending = None
        pending = (x2, x.shape)
    outs.append(_rmsnorm(pending[0], final_norm, F32, "final_rmsnorm").reshape(pending[1]))
    return tuple(outs)
```

```python
import functools
import math

import jax
import jax.numpy as jnp
from jax import lax
from jax.experimental import pallas as pl
from jax.experimental.pallas import tpu as pltpu

F32 = jnp.float32
BF16 = jnp.bfloat16

NORM_EPS = 1e-6
HEAD_DIM = 128
HEADS_PER_GROUP = 8
GROUP_WIDTH = HEADS_PER_GROUP * HEAD_DIM
DILATED_GROUPS = ((128, 1), (512, 4), (2048, 16))
N_GROUPS = len(DILATED_GROUPS)
DILATIONS = tuple(sorted({d for _, d in DILATED_GROUPS}))
ROPE_DIM = HEAD_DIM // 4
ROPE_HALF = ROPE_DIM // 2
ROPE_THETA = 500000.0
NEG_INF = -1e30
LOG2_E = math.log2(math.e)
SGU_CHUNK = 128
SGU_GROUPS = 16
LANES = 128
BF16_SUBLANES = 16
WEIGHT_COL_TILE = 512
OUT_PROJ_COL_TILE = 1024
EPILOGUE_CHUNK_ROWS = 128
V7X_VMEM_BYTES = 64 * 1024 * 1024
VMEM_LIMIT_BYTES = V7X_VMEM_BYTES - 8 * 1024 * 1024


def _params():
    return pltpu.CompilerParams(vmem_limit_bytes=VMEM_LIMIT_BYTES)


def _to_residue_major(rows, dil):
    n = rows // dil
    i = lax.broadcasted_iota(jnp.int32, (rows, rows), 0)
    j = lax.broadcasted_iota(jnp.int32, (rows, rows), 1)
    src = (i % n) * dil + i // n
    return jnp.where(j == src, 1.0, 0.0).astype(BF16)


def _to_natural(rows, dil):
    n = rows // dil
    i = lax.broadcasted_iota(jnp.int32, (rows, rows), 0)
    j = lax.broadcasted_iota(jnp.int32, (rows, rows), 1)
    src = (i % dil) * n + i // dil
    return jnp.where(j == src, 1.0, 0.0).astype(BF16)


def _rmsnorm_kernel(x_ref, g_ref, o_ref):
    x = x_ref[...]
    y = x * lax.rsqrt(jnp.mean(x * x, axis=-1, keepdims=True) + NORM_EPS)
    o_ref[...] = (y * g_ref[...]).astype(o_ref.dtype)


def _rmsnorm(x, g, out_dtype, name):
    t, d = x.shape
    tr = 256
    return pl.pallas_call(
        _rmsnorm_kernel,
        grid=(t // tr,),
        in_specs=[pl.BlockSpec((tr, d), lambda i: (i, 0)),
                  pl.BlockSpec((1, d), lambda i: (0, 0))],
        out_specs=pl.BlockSpec((tr, d), lambda i: (i, 0)),
        out_shape=jax.ShapeDtypeStruct((t, d), out_dtype),
        compiler_params=_params(),
        name=name,
    )(x, g.reshape(1, d))


def _rmsnorm_orders_kernel(x_ref, g_ref, *o_refs):
    rows = x_ref.shape[0]
    x = x_ref[...]
    y = x * lax.rsqrt(jnp.mean(x * x, axis=-1, keepdims=True) + NORM_EPS)
    y = (y * g_ref[...]).astype(BF16)
    for dil, o_ref in zip(DILATIONS, o_refs):
        if dil == 1:
            o_ref[...] = y
        else:
            n = rows // dil
            yp = jnp.dot(_to_residue_major(rows, dil), y, preferred_element_type=F32).astype(BF16)
            for r in range(dil):
                o_ref[r] = yp[r * n:(r + 1) * n, :]


def _rmsnorm_orders(x2, g, batch, seq):
    t, d = x2.shape
    tr = BF16_SUBLANES * max(DILATIONS)
    per_seq = seq // tr
    out_specs, out_shape = [], []
    for dil in DILATIONS:
        if dil == 1:
            out_specs.append(pl.BlockSpec((tr, d), lambda i: (i, 0)))
            out_shape.append(jax.ShapeDtypeStruct((t, d), BF16))
        else:
            out_specs.append(pl.BlockSpec((None, dil, tr // dil, d),
                                          lambda i: (i // per_seq, 0, i % per_seq, 0)))
            out_shape.append(jax.ShapeDtypeStruct((batch, dil, seq // dil, d), BF16))
    outs = pl.pallas_call(
        _rmsnorm_orders_kernel,
        grid=(t // tr,),
        in_specs=[pl.BlockSpec((tr, d), lambda i: (i, 0)),
                  pl.BlockSpec((1, d), lambda i: (0, 0))],
        out_specs=out_specs,
        out_shape=out_shape,
        compiler_params=_params(),
        name="attn_rmsnorm",
    )(x2, g.reshape(1, d))
    return {dil: o.reshape(t, d) for dil, o in zip(DILATIONS, outs)}


def _rope_table_kernel(f_ref, c_ref, sa_ref, sb_ref):
    rows = c_ref.shape[0]
    pos = (pl.program_id(0) * rows + lax.broadcasted_iota(jnp.int32, (rows, LANES), 0)).astype(F32)
    lane = lax.broadcasted_iota(jnp.int32, (rows, LANES), 1)
    ang = pos * f_ref[...]
    sin = jnp.sin(ang)
    c_ref[...] = jnp.cos(ang)
    sa_ref[...] = jnp.where(lane < ROPE_HALF, -sin, 0.0)
    sb_ref[...] = jnp.where((lane >= ROPE_HALF) & (lane < ROPE_DIM), sin, 0.0)


def _rope_reorder_kernel(*refs, plan):
    nat, outs = refs[:3], refs[3:]
    for k, (seq, dil) in enumerate(plan):
        n = seq // dil
        for r in range(dil):
            for src, dst in zip(nat, outs[3 * k:3 * k + 3]):
                dst[r * n:(r + 1) * n, :] = src[pl.ds(r, n, stride=dil), :]


def _rope_tables(seqs):
    inv_freq = ROPE_THETA ** (-jnp.arange(0, ROPE_DIM, 2, dtype=F32) / ROPE_DIM)
    lane_freq = jnp.concatenate([inv_freq, inv_freq, jnp.zeros((LANES - ROPE_DIM,), F32)]).reshape(1, LANES)
    s_max = max(seqs)
    rows = 256
    spec = pl.BlockSpec((rows, LANES), lambda i: (i, 0))
    shape = jax.ShapeDtypeStruct((s_max, LANES), F32)
    nat = pl.pallas_call(
        _rope_table_kernel,
        grid=(s_max // rows,),
        in_specs=[pl.BlockSpec((1, LANES), lambda i: (0, 0))],
        out_specs=[spec, spec, spec],
        out_shape=[shape, shape, shape],
        name="rope_tables",
    )(lane_freq)
    plan = [(seq, dil) for seq in seqs for dil in DILATIONS if dil != 1]
    reordered = pl.pallas_call(
        functools.partial(_rope_reorder_kernel, plan=plan),
        out_shape=[jax.ShapeDtypeStruct((seq, LANES), F32) for seq, _ in plan for _ in range(3)],
        compiler_params=_params(),
        name="rope_reorder",
    )(*nat)
    tables = {(seq, 1): tuple(nat) for seq in seqs}
    for k, key in enumerate(plan):
        tables[key] = tuple(reordered[3 * k:3 * k + 3])
    return tables


PROJ_CHUNK_ROWS = {"rope": EPILOGUE_CHUNK_ROWS, "plain": 1024, "gelu": EPILOGUE_CHUNK_ROWS,
                   "sigmoid": EPILOGUE_CHUNK_ROWS}


def _gelu(x):
    return 0.5 * x * (1.0 + lax.erf(x * math.sqrt(0.5)))


def _rider_specs(weights, grid):
    steps = grid[0] * grid[1]
    in_specs, out_specs, out_shape = [], [], []
    for w, col_tile in weights:
        rows, cols = w.shape
        n = max(c for c in range(1, steps + 1) if rows % c == 0 and (rows // c) % BF16_SUBLANES == 0)
        in_specs.append(pl.BlockSpec((rows // n, cols), lambda i, j, n=n: (jnp.minimum(i * grid[1] + j, n - 1), 0)))
        if col_tile is None:
            out_specs.append(in_specs[-1])
            out_shape.append(jax.ShapeDtypeStruct((rows, cols), BF16))
        else:
            out_specs.append(pl.BlockSpec((cols // col_tile, rows // n, col_tile),
                                          lambda i, j, n=n: (0, jnp.minimum(i * grid[1] + j, n - 1), 0)))
            out_shape.append(jax.ShapeDtypeStruct((cols // col_tile, rows, col_tile), BF16))
    return in_specs, out_specs, out_shape


def _cast_riders(in_refs, out_refs):
    for src, dst in zip(in_refs, out_refs):
        if len(dst.shape) == 2:
            dst[...] = src[...].astype(dst.dtype)
        else:
            ct = dst.shape[2]
            for c in range(dst.shape[0]):
                dst[c] = src[:, c * ct:(c + 1) * ct].astype(dst.dtype)


def _proj_kernel(*refs, kind, cm, n_riders, norm_rider):
    n_in = 2 + (3 if kind == "rope" else 0)
    n_all_in = n_in + n_riders + (2 if norm_rider else 0)
    a_ref, w_ref = refs[:2]
    o_ref = refs[n_all_in]
    _cast_riders(refs[n_in:n_in + n_riders], refs[n_all_in + 1:n_all_in + 1 + n_riders])
    if norm_rider:
        _rmsnorm_kernel(refs[n_all_in - 2], refs[n_all_in - 1], refs[-1])
    for r0 in range(0, a_ref.shape[0], cm):
        rows = slice(r0, r0 + cm)
        acc = jnp.dot(a_ref[rows, :], w_ref[...], preferred_element_type=F32)
        if kind == "rope":
            c, sa, sb = (t_ref[rows, :] for t_ref in refs[2:n_in])
            is_v = pl.program_id(1) == 2
            c, sa, sb = jnp.where(is_v, 1.0, c), jnp.where(is_v, 0.0, sa), jnp.where(is_v, 0.0, sb)
            for h in range(acc.shape[1] // HEAD_DIM):
                sl = slice(h * HEAD_DIM, (h + 1) * HEAD_DIM)
                z = acc[:, sl]
                rot = z * c + pltpu.roll(z, HEAD_DIM - ROPE_HALF, 1) * sa + pltpu.roll(z, ROPE_HALF, 1) * sb
                o_ref[rows, sl] = rot.astype(o_ref.dtype)
        elif kind == "gelu":
            o_ref[rows, :] = _gelu(acc).astype(o_ref.dtype)
        elif kind == "sigmoid":
            o_ref[rows, :] = jax.nn.sigmoid(acc).astype(o_ref.dtype)
        else:
            o_ref[rows, :] = acc.astype(o_ref.dtype)


def _proj(a, w, first, step, count, kind, name, tables=None, seq=None, riders=(), norm_rider=None):
    m, k = a.shape
    tm, tn = 1024, GROUP_WIDTH
    grid = (m // tm, count)
    in_specs = [pl.BlockSpec((tm, k), lambda i, j: (i, 0)),
                pl.BlockSpec((k, tn), lambda i, j: (0, first + step * j))]
    args = [a, w]
    if kind == "rope":
        per_seq = seq // tm
        tspec = pl.BlockSpec((tm, LANES), lambda i, j: (i % per_seq, 0))
        in_specs += [tspec, tspec, tspec]
        args += list(tables)
    r_in, r_out, r_shape = _rider_specs(riders, grid)
    n_in, n_out, extra = [], [], []
    if norm_rider is not None:
        y, g = norm_rider
        rows = y.shape[0] // (grid[0] * grid[1])
        yspec = pl.BlockSpec((rows, y.shape[1]), lambda i, j: (i * count + j, 0))
        n_in = [yspec, pl.BlockSpec((1, y.shape[1]), lambda i, j: (0, 0))]
        n_out = [yspec]
        extra = [y, g.reshape(1, -1)]
    out, *rest = pl.pallas_call(
        functools.partial(_proj_kernel, kind=kind, cm=PROJ_CHUNK_ROWS[kind], n_riders=len(riders),
                          norm_rider=norm_rider is not None),
        grid=grid,
        in_specs=in_specs + r_in + n_in,
        out_specs=[pl.BlockSpec((tm, tn), lambda i, j: (i, j))] + r_out + n_out,
        out_shape=[jax.ShapeDtypeStruct((m, count * tn), BF16)] + r_shape
        + ([jax.ShapeDtypeStruct(norm_rider[0].shape, F32)] if norm_rider is not None else []),
        compiler_params=_params(),
        name=name,
    )(*args, *(w for w, _ in riders), *extra)
    normed = rest.pop() if norm_rider is not None else None
    return out, rest, normed


def _attn_kernel(q_ref, k_ref, v_ref, o_ref, m_ref, l_ref, *, qs, win, sub_len, radius):
    tq = q_ref.shape[0]
    kv_rows = k_ref.shape[0]
    rel = (lax.broadcasted_iota(jnp.int32, (qs, win), 0) - lax.broadcasted_iota(jnp.int32, (qs, win), 1))
    scale = 1.0 / math.sqrt(HEAD_DIM)
    head_slices = [slice(h * HEAD_DIM, (h + 1) * HEAD_DIM) for h in range(HEADS_PER_GROUP)]
    m_ref[...] = jnp.zeros_like(m_ref)
    l_ref[...] = jnp.zeros_like(l_ref)
    for qb in range(tq // qs):
        rows = slice(qb * qs, (qb + 1) * qs)
        if kv_rows == sub_len:
            l0 = pl.program_id(1) * tq + qb * qs
            start = pl.multiple_of(jnp.clip(l0 - radius, 0, sub_len - win), radius)
            kv = pl.ds(start, win)
        else:
            l0 = (qb * qs) % sub_len
            start = min(max(l0 - radius, 0), sub_len - win)
            kv = pl.ds((qb * qs) // sub_len * sub_len + start, win)
        scores = [lax.dot_general(q_ref[rows, sl], k_ref[kv, sl], (((1,), (1,)), ((), ())),
                                  preferred_element_type=F32) for sl in head_slices]
        valid = jnp.abs(rel + (l0 - start)) <= radius
        for h, sl in enumerate(head_slices):
            s = jnp.where(valid, scores[h], NEG_INF)
            m = jnp.max(s, axis=-1, keepdims=True)
            p = jnp.exp2((s - m) * (scale * LOG2_E))
            l = jnp.sum(p, axis=-1, keepdims=True)
            o = jnp.dot(p.astype(BF16), v_ref[kv, sl], preferred_element_type=F32)
            o_ref[rows, sl] = o.astype(o_ref.dtype)
            m_ref[rows, h:h + 1] = m * scale
            l_ref[rows, h:h + 1] = l


def _attn_group(qkv, seq, g):
    t = qkv.shape[0]
    window, dil = DILATED_GROUPS[g]
    radius = window // (2 * dil)
    sub_len = seq // dil
    tq = 1024
    qs = min(128, sub_len)
    win = min(qs + 2 * radius, sub_len)
    if sub_len >= tq:
        per_sub = sub_len // tq
        grid = (t // sub_len, per_sub)
        kv_rows = sub_len
        q_map = lambda s, j: (s * per_sub + j, 0)
        k_map = lambda s, j: (s, 1)
        v_map = lambda s, j: (s, 2)
    else:
        grid = (t // tq, 1)
        kv_rows = tq
        q_map = lambda s, j: (s, 0)
        k_map = lambda s, j: (s, 1)
        v_map = lambda s, j: (s, 2)
    return pl.pallas_call(
        functools.partial(_attn_kernel, qs=qs, win=win, sub_len=sub_len, radius=radius),
        grid=grid,
        in_specs=[
            pl.BlockSpec((tq, GROUP_WIDTH), q_map),
            pl.BlockSpec((kv_rows, GROUP_WIDTH), k_map),
            pl.BlockSpec((kv_rows, GROUP_WIDTH), v_map),
        ],
        out_specs=[
            pl.BlockSpec((tq, GROUP_WIDTH), q_map),
            pl.BlockSpec((tq, LANES), q_map),
            pl.BlockSpec((tq, LANES), q_map),
        ],
        out_shape=[
            jax.ShapeDtypeStruct((t, GROUP_WIDTH), BF16),
            jax.ShapeDtypeStruct((t, LANES), F32),
            jax.ShapeDtypeStruct((t, LANES), F32),
        ],
        compiler_params=_params(),
        name=f"attn_g{g}",
    )(qkv, qkv, qkv)


def _combine_kernel(*refs):
    o_refs, m_refs, l_refs = refs[0:N_GROUPS], refs[N_GROUPS:2 * N_GROUPS], refs[2 * N_GROUPS:3 * N_GROUPS]
    out_ref, m_scr, l_scr = refs[3 * N_GROUPS:]
    rows = out_ref.shape[0]
    os_, ms, ls = [], [], []
    for g, (_, dil) in enumerate(DILATED_GROUPS):
        if dil == 1:
            os_.append(o_refs[g][...].astype(F32))
            ms.append(m_refs[g][...])
            ls.append(l_refs[g][...])
        else:
            n = rows // dil
            for r in range(dil):
                m_scr[g, pl.ds(r, n, stride=dil), :] = m_refs[g][r]
                l_scr[g, pl.ds(r, n, stride=dil), :] = l_refs[g][r]
            ms.append(m_scr[g])
            ls.append(l_scr[g])
            og = o_refs[g][...].reshape(rows, GROUP_WIDTH)
            os_.append(jnp.dot(_to_natural(rows, dil), og, preferred_element_type=F32))
    m_max = functools.reduce(jnp.maximum, ms)
    ws = [jnp.exp(m - m_max) for m in ms]
    den = sum(w * l for w, l in zip(ws, ls))
    coef = [w / den for w in ws]
    for h in range(HEADS_PER_GROUP):
        sl = slice(h * HEAD_DIM, (h + 1) * HEAD_DIM)
        out_ref[:, sl] = sum(c[:, h:h + 1] * o[:, sl] for c, o in zip(coef, os_)).astype(out_ref.dtype)


def _combine(parts, batch, seq):
    t = batch * seq
    tr = BF16_SUBLANES * 2 * max(DILATIONS)
    per_seq = seq // tr
    o_specs, s_specs, o_args, m_args, l_args = [], [], [], [], []
    for (o, m, l), (_, dil) in zip(parts, DILATED_GROUPS):
        if dil == 1:
            o_specs.append(pl.BlockSpec((tr, GROUP_WIDTH), lambda i: (i, 0)))
            s_specs.append(pl.BlockSpec((tr, LANES), lambda i: (i, 0)))
            o_args.append(o), m_args.append(m), l_args.append(l)
        else:
            imap = lambda i: (i // per_seq, 0, i % per_seq, 0)
            o_specs.append(pl.BlockSpec((None, dil, tr // dil, GROUP_WIDTH), imap))
            s_specs.append(pl.BlockSpec((None, dil, tr // dil, LANES), imap))
            o_args.append(o.reshape(batch, dil, seq // dil, GROUP_WIDTH))
            m_args.append(m.reshape(batch, dil, seq // dil, LANES))
            l_args.append(l.reshape(batch, dil, seq // dil, LANES))
    return pl.pallas_call(
        _combine_kernel,
        grid=(t // tr,),
        in_specs=o_specs + s_specs + s_specs,
        out_specs=pl.BlockSpec((tr, GROUP_WIDTH), lambda i: (i, 0)),
        out_shape=jax.ShapeDtypeStruct((t, GROUP_WIDTH), BF16),
        scratch_shapes=[pltpu.VMEM((N_GROUPS, tr, LANES), F32), pltpu.VMEM((N_GROUPS, tr, LANES), F32)],
        compiler_params=_params(),
        name="attn_combine",
    )(*o_args, *m_args, *l_args)


def _sgu_kernel(u_ref, vs_ref, g_ref, w_ref, bt_ref, o_ref):
    rows, width = vs_ref.shape
    gdim = width // SGU_GROUPS
    gv = vs_ref[...].astype(F32)
    vn = gv * lax.rsqrt(jnp.mean(gv * gv, axis=-1, keepdims=True) + NORM_EPS)
    vn = (vn * g_ref[...]).astype(BF16)
    for g in range(SGU_GROUPS):
        w = w_ref[g].astype(BF16)
        bias = bt_ref[:, g:g + 1]
        cs = slice(g * gdim, (g + 1) * gdim)
        for c in range(rows // SGU_CHUNK):
            rs = slice(c * SGU_CHUNK, (c + 1) * SGU_CHUNK)
            sp = jnp.dot(w, vn[rs, cs], preferred_element_type=F32) + bias
            o_ref[rs, cs] = (u_ref[rs, cs].astype(F32) * sp).astype(o_ref.dtype)


def _sgu(uv, sgu_norm, sgu_w, sgu_b):
    t = uv.shape[0]
    width = sgu_norm.shape[0]
    tr = 4 * SGU_CHUNK
    return pl.pallas_call(
        _sgu_kernel,
        grid=(t // tr,),
        in_specs=[
            pl.BlockSpec((tr, width), lambda i: (i, 0)),
            pl.BlockSpec((tr, width), lambda i: (i, 1)),
            pl.BlockSpec((1, width), lambda i: (0, 0)),
            pl.BlockSpec((SGU_GROUPS, SGU_CHUNK, SGU_CHUNK), lambda i: (0, 0, 0)),
            pl.BlockSpec((SGU_CHUNK, SGU_GROUPS), lambda i: (0, 0)),
        ],
        out_specs=pl.BlockSpec((tr, width), lambda i: (i, 0)),
        out_shape=jax.ShapeDtypeStruct((t, width), BF16),
        compiler_params=_params(),
        name="sgu",
    )(uv, uv, sgu_norm.reshape(1, width), sgu_w, sgu_b.T)


def _merge_kernel(at_ref, sg_ref, w_ref, ga_ref, gb_ref, *rest, n_riders):
    o_ref = rest[n_riders]
    _cast_riders(rest[:n_riders], rest[n_riders + 1:])
    ka = at_ref.shape[1]
    for r0 in range(0, at_ref.shape[0], EPILOGUE_CHUNK_ROWS):
        rows = slice(r0, r0 + EPILOGUE_CHUNK_ROWS)
        a = jnp.dot(at_ref[rows, :], w_ref[:ka, :], preferred_element_type=F32)
        b = jnp.dot(sg_ref[rows, :], w_ref[ka:, :], preferred_element_type=F32)
        o_ref[rows, :] = (ga_ref[rows, :].astype(F32) * a + gb_ref[rows, :].astype(F32) * b).astype(o_ref.dtype)


def _merge(attn, sgu, w_branch, gates, riders=()):
    t = attn.shape[0]
    n_tiles, k, tn = w_branch.shape
    tm, d = 1024, n_tiles * tn
    grid = (t // tm, n_tiles)
    r_in, r_out, r_shape = _rider_specs(riders, grid)
    out, *cast = pl.pallas_call(
        functools.partial(_merge_kernel, n_riders=len(riders)),
        grid=grid,
        in_specs=[
            pl.BlockSpec((tm, attn.shape[1]), lambda i, j: (i, 0)),
            pl.BlockSpec((tm, sgu.shape[1]), lambda i, j: (i, 0)),
            pl.BlockSpec((None, k, tn), lambda i, j: (j, 0, 0)),
            pl.BlockSpec((tm, tn), lambda i, j: (i, j)),
            pl.BlockSpec((tm, tn), lambda i, j: (i, d // tn + j)),
        ] + r_in,
        out_specs=[pl.BlockSpec((tm, tn), lambda i, j: (i, j))] + r_out,
        out_shape=[jax.ShapeDtypeStruct((t, d), BF16)] + r_shape,
        compiler_params=_params(),
        name="branch_merge",
    )(attn, sgu, w_branch, gates, gates, *(w for w, _ in riders))
    return out, cast


def _out_proj_kernel(a_ref, w_ref, x_ref, g_ref, x1_ref, xg_ref, ss_ref):
    @pl.when(pl.program_id(1) == 0)
    def _():
        ss_ref[...] = jnp.zeros_like(ss_ref)

    for r0 in range(0, a_ref.shape[0], EPILOGUE_CHUNK_ROWS):
        rows = slice(r0, r0 + EPILOGUE_CHUNK_ROWS)
        x1 = x_ref[rows, :] + jnp.dot(a_ref[rows, :], w_ref[...], preferred_element_type=F32)
        x1_ref[rows, :] = x1
        xg_ref[rows, :] = (x1 * g_ref[...]).astype(xg_ref.dtype)
        ss_ref[rows, :] += jnp.sum(x1 * x1, axis=-1, keepdims=True)


def _out_proj(a, w, x, g):
    t, k = a.shape
    n_tiles, _, tn = w.shape
    tm, d = 1024, n_tiles * tn
    return pl.pallas_call(
        _out_proj_kernel,
        grid=(t // tm, n_tiles),
        in_specs=[
            pl.BlockSpec((tm, k), lambda i, j: (i, 0)),
            pl.BlockSpec((None, k, tn), lambda i, j: (j, 0, 0)),
            pl.BlockSpec((tm, tn), lambda i, j: (i, j)),
            pl.BlockSpec((1, tn), lambda i, j: (0, j)),
        ],
        out_specs=[
            pl.BlockSpec((tm, tn), lambda i, j: (i, j)),
            pl.BlockSpec((tm, tn), lambda i, j: (i, j)),
            pl.BlockSpec((tm, LANES), lambda i, j: (i, 0)),
        ],
        out_shape=[
            jax.ShapeDtypeStruct((t, d), F32),
            jax.ShapeDtypeStruct((t, d), BF16),
            jax.ShapeDtypeStruct((t, LANES), F32),
        ],
        compiler_params=_params(),
        name="out_proj",
    )(a, w, x, g.reshape(1, d))


def _gate_up_kernel(xg_ref, ss_ref, wg_ref, wu_ref, o_ref, *, d_model):
    for r0 in range(0, xg_ref.shape[0], EPILOGUE_CHUNK_ROWS):
        rows = slice(r0, r0 + EPILOGUE_CHUNK_ROWS)
        inv_rms = lax.rsqrt(ss_ref[rows, 0:1] / d_model + NORM_EPS)
        xg = xg_ref[rows, :]
        g = inv_rms * jnp.dot(xg, wg_ref[...], preferred_element_type=F32)
        u = inv_rms * jnp.dot(xg, wu_ref[...], preferred_element_type=F32)
        o_ref[rows, :] = (jax.nn.silu(g) * u).astype(o_ref.dtype)


def _gate_up(xg, ss, wg, wu):
    t, k = xg.shape
    f = wg.shape[1]
    tm, tn = 2048, 256
    return pl.pallas_call(
        functools.partial(_gate_up_kernel, d_model=k),
        grid=(t // tm, f // tn),
        in_specs=[
            pl.BlockSpec((tm, k), lambda i, j: (i, 0)),
            pl.BlockSpec((tm, LANES), lambda i, j: (i, 0)),
            pl.BlockSpec((k, tn), lambda i, j: (0, j)),
            pl.BlockSpec((k, tn), lambda i, j: (0, j)),
        ],
        out_specs=pl.BlockSpec((tm, tn), lambda i, j: (i, j)),
        out_shape=jax.ShapeDtypeStruct((t, f), BF16),
        compiler_params=_params(),
        name="ffn_gate_up",
    )(xg, ss, wg, wu)


def _down_kernel(a_ref, w_ref, x_ref, o_ref):
    o_ref[...] = x_ref[...] + jnp.dot(a_ref[...], w_ref[...], preferred_element_type=F32)


def _down(a, w, x):
    t, k = a.shape
    n_tiles, _, tn = w.shape
    tm, d = 512, n_tiles * tn
    return pl.pallas_call(
        _down_kernel,
        grid=(t // tm, n_tiles),
        in_specs=[
            pl.BlockSpec((tm, k), lambda i, j: (i, 0)),
            pl.BlockSpec((None, k, tn), lambda i, j: (j, 0, 0)),
            pl.BlockSpec((tm, tn), lambda i, j: (i, j)),
        ],
        out_specs=pl.BlockSpec((tm, tn), lambda i, j: (i, j)),
        out_shape=jax.ShapeDtypeStruct((t, d), F32),
        compiler_params=_params(),
        name="ffn_down",
    )(a, w, x)


def _layer(x2, w, mxu_w, tables, batch, seq, norm_rider=None):
    d = x2.shape[1]
    sgu_width = w["sgu_norm"].shape[0]
    uv_first = 3 * N_GROUPS
    gates_first = uv_first + 2 * sgu_width // GROUP_WIDTH
    cast = mxu_w is None

    hs = _rmsnorm_orders(x2, w["attn_norm"], batch, seq)
    parts = []
    for g, (_, dil) in enumerate(DILATED_GROUPS):
        qkv, _, _ = _proj(hs[dil], w["w_in"], g, N_GROUPS, 3, "rope", f"proj_qkv{g}",
                          tables=tables[(seq, dil)], seq=seq)
        parts.append(_attn_group(qkv, seq, g))
    attn = _combine(parts, batch, seq)

    uv, c_uv, normed = _proj(
        hs[1], w["w_in"], uv_first, 1, 2 * sgu_width // GROUP_WIDTH, "gelu", "proj_uv",
        riders=((w["w_gate"], None), (w["w_out"], OUT_PROJ_COL_TILE)) if cast else (), norm_rider=norm_rider)
    gates, c_gates, _ = _proj(
        hs[1], w["w_in"], gates_first, 1, 2 * d // GROUP_WIDTH, "sigmoid", "proj_gates",
        riders=((w["w_up"], None), (w["w_branch"], WEIGHT_COL_TILE)) if cast else ())
    if cast:
        mxu_w = {"w_gate": c_uv[0], "w_out": c_uv[1], "w_up": c_gates[0], "w_branch": c_gates[1]}
    sgu = _sgu(uv, w["sgu_norm"], w["sgu_w"], w["sgu_b"])
    merged, c_merge = _merge(attn, sgu, mxu_w["w_branch"], gates,
                             riders=((w["w_down"], WEIGHT_COL_TILE),) if cast else ())
    if cast:
        mxu_w["w_down"] = c_merge[0]
    x1, xg, ss = _out_proj(merged, mxu_w["w_out"], x2, w["ffn_norm"])

    gu = _gate_up(xg, ss, mxu_w["w_gate"], mxu_w["w_up"])
    return _down(gu, mxu_w["w_down"], x1), mxu_w, normed


def kernel(x_prompt, x_sample, attn_norm, w_in, sgu_norm, sgu_w, sgu_b, w_branch, w_out,
           ffn_norm, w_gate, w_up, w_down, final_norm):
    depth = w_in.shape[0]
    layers = [
        {
            "attn_norm": attn_norm[i], "w_in": w_in[i].astype(BF16),
            "sgu_norm": sgu_norm[i], "sgu_w": sgu_w[i], "sgu_b": sgu_b[i],
            "w_branch": w_branch[i], "w_out": w_out[i], "ffn_norm": ffn_norm[i],
            "w_gate": w_gate[i], "w_up": w_up[i], "w_down": w_down[i],
        }
        for i in range(depth)
    ]
    mxu_weights = [None] * depth
    tables = _rope_tables(sorted({x_prompt.shape[1], x_sample.shape[1]}))
    outs = []
    pending = None
    for x in (x_prompt, x_sample):
        batch, seq, d = x.shape
        x2 = x.reshape(batch * seq, d)
        for i, w in enumerate(layers):
            rider = (pending[0], final_norm) if pending is not None and i == 0 else None
            x2, mxu_weights[i], normed = _layer(x2, w, mxu_weights[i], tables, batch, seq, norm_rider=rider)
            if rider is not None:
                outs.append(normed.reshape(pending[1]))
                pending = None
        pending = (x2, x.shape)
    outs.append(_rmsnorm(pending[0], final_norm, F32, "final_rmsnorm").reshape(pending[1]))
    return tuple(outs)
```
